```python
import math
import jax, jax.numpy as jnp
from jax import lax
import numpy as np

D_MODEL = 2048
BATCH = 1
SEQ = 16384
DEPTH = 2

CHUNK = 64
N_MEM = 256
EPS = 1e-6

D_MIX = D_MODEL
D_SSM = D_MIX // 2
D_ATT = D_MIX - D_SSM
SSM_GROUP = 16
N_SSM_GROUPS = D_SSM // SSM_GROUP
SSM_STATE = 64
ATT_HEAD_DIM = 64
N_ATT_HEADS = D_ATT // (2 * ATT_HEAD_DIM)
ATT_V_DIM = 2 * ATT_HEAD_DIM
D_IN = D_SSM + 3 * D_ATT
Q_BLOCK = 128

N_MEM_HEADS = 4
MEM_HEAD_DIM = D_MODEL // N_MEM_HEADS

PEER_HEADS = 8
PEER_N_KEYS = 128
PEER_N_EXPERTS = PEER_N_KEYS * PEER_N_KEYS
PEER_D_HALF = 128
PEER_D_QUERY = 2 * PEER_D_HALF
PEER_TOPK = 16
TOKEN_BLOCK = 128

kernel_name = "hybrid_s5_diffattn_peer_encoder"


def rmsnorm(x, g):
    xf = x.astype(jnp.float32)
    y = xf * lax.rsqrt(jnp.mean(xf * xf, axis=-1, keepdims=True) + EPS)
    return (y * g.astype(jnp.float32)).astype(x.dtype)


def _lin_rec(c1, c2):
    a1, b1 = c1
    a2, b2 = c2
    return a1 * a2, a2 * b1 + b2


def s5_mixer(u, a_re, a_im, log_step, b_re, b_im, c_re, c_im, d_skip, w_glu, b_glu):
    bsz, s, _ = u.shape
    uf = u.astype(jnp.float32).reshape(bsz, s, N_SSM_GROUPS, SSM_GROUP)
    A = lax.complex(a_re.astype(jnp.float32), a_im.astype(jnp.float32))
    step = jnp.exp(log_step.astype(jnp.float32))[:, None]
    a_bar = jnp.exp(A * step)
    Bm = lax.complex(b_re.astype(jnp.float32), b_im.astype(jnp.float32))
    Cm = lax.complex(c_re.astype(jnp.float32), c_im.astype(jnp.float32))
    b_bar = ((a_bar - 1.0) / A)[..., None] * Bm
    bu = jnp.einsum('bsgc,gpc->bsgp', uf.astype(jnp.complex64), b_bar)
    a_seq = jnp.broadcast_to(a_bar, bu.shape)
    _, states = lax.associative_scan(_lin_rec, (a_seq, bu), axis=1)
    y = jnp.real(jnp.einsum('bsgp,gcp->bsgc', states, Cm))
    y = y + d_skip.astype(jnp.float32).reshape(N_SSM_GROUPS, SSM_GROUP) * uf
    y = jax.nn.gelu(y.reshape(bsz, s, D_SSM))
    y = y * jax.nn.sigmoid(y @ w_glu.astype(jnp.float32) + b_glu.astype(jnp.float32))
    return y.astype(u.dtype)


def diff_attention(q, k, v, lam, lam_init, subln_g):
    bsz, s = q.shape[0], q.shape[1]
    nb = s // Q_BLOCK
    scale = ATT_HEAD_DIM ** -0.5
    kf = k.astype(jnp.float32)
    vf = v.astype(jnp.float32)
    k_chunk = jnp.arange(s) // CHUNK
    qb = q.astype(jnp.float32).reshape(bsz, nb, Q_BLOCK, N_ATT_HEADS, 2, ATT_HEAD_DIM)
    qb = qb.transpose(1, 0, 2, 3, 4, 5)

    def block(args):
        qblk, bi = args
        q_chunk = (bi * Q_BLOCK + jnp.arange(Q_BLOCK)) // CHUNK
        mask = k_chunk[None, :] <= q_chunk[:, None]
        sc = jnp.einsum('bqhid,bkhid->bhiqk', qblk, kf) * scale
        sc = jnp.where(mask, sc, -jnp.inf)
        p = jax.nn.softmax(sc, axis=-1)
        w = p[:, :, 0] - lam * p[:, :, 1]
        return jnp.einsum('bhqk,bkhe->bqhe', w, vf)

    o = lax.map(block, (qb, jnp.arange(nb)))
    o = o.transpose(1, 0, 2, 3, 4).reshape(bsz, s, N_ATT_HEADS, ATT_V_DIM)
    o = rmsnorm(o, subln_g) * (1.0 - lam_init)
    return o.reshape(bsz, s, D_ATT).astype(q.dtype)


def mem_cross_attention(x, mem_n, w_q, w_k, w_v, w_o):
    bsz, s, _ = x.shape
    m = mem_n.shape[1]
    q = (x @ w_q).reshape(bsz, s, N_MEM_HEADS, MEM_HEAD_DIM)
    k = (mem_n @ w_k).reshape(bsz, m, N_MEM_HEADS, MEM_HEAD_DIM)
    v = (mem_n @ w_v).reshape(bsz, m, N_MEM_HEADS, MEM_HEAD_DIM)
    sc = jnp.einsum('bshe,bmhe->bhsm', q, k).astype(jnp.float32) * (MEM_HEAD_DIM ** -0.5)
    p = jax.nn.softmax(sc, axis=-1)
    o = jnp.einsum('bhsm,bmhe->bshe', p, v.astype(jnp.float32)).astype(x.dtype)
    return o.reshape(bsz, s, D_MODEL) @ w_o


def peer_ffn(x, w_q, sub_k1, sub_k2, u_tab, v_tab):
    bsz, s, d = x.shape
    nb = s // TOKEN_BLOCK
    xb = x.reshape(bsz, nb, TOKEN_BLOCK, d).transpose(1, 0, 2, 3)

    def block(xblk):
        q = (xblk @ w_q).reshape(bsz, TOKEN_BLOCK, PEER_HEADS, 2, PEER_D_HALF)
        s1 = jnp.einsum('bthd,hkd->bthk', q[..., 0, :], sub_k1).astype(jnp.float32)
        s2 = jnp.einsum('bthd,hkd->bthk', q[..., 1, :], sub_k2).astype(jnp.float32)
        v1, i1 = lax.top_k(s1, PEER_TOPK)
        v2, i2 = lax.top_k(s2, PEER_TOPK)
        cand = (v1[..., :, None] + v2[..., None, :]).reshape(bsz, TOKEN_BLOCK, PEER_HEADS, PEER_TOPK * PEER_TOPK)
        sc, ci = lax.top_k(cand, PEER_TOPK)
        e1 = jnp.take_along_axis(i1, ci // PEER_TOPK, axis=-1)
        e2 = jnp.take_along_axis(i2, ci % PEER_TOPK, axis=-1)
        idx = e1 * PEER_N_KEYS + e2
        g = jax.nn.softmax(sc, axis=-1)
        u = jnp.take(u_tab, idx, axis=0)
        a = jnp.einsum('btd,bthkd->bthk', xblk, u).astype(jnp.float32)
        h = (jax.nn.gelu(a) * g).astype(x.dtype)
        vv = jnp.take(v_tab, idx, axis=0)
        return jnp.einsum('bthk,bthkd->btd', h, vv)

    out = lax.map(block, xb)
    return out.transpose(1, 0, 2, 3).reshape(bsz, s, d)


def setup_inputs(seed: int = 0) -> dict:
    key = jax.random.key(seed)
    ks = jax.random.split(key, 40)
    f32 = jnp.float32
    L, D, G, P, C = DEPTH, D_MODEL, N_SSM_GROUPS, SSM_STATE, SSM_GROUP

    def nrm(i, shape, scale):
        return jax.random.normal(ks[i], shape, f32) * scale

    def gain(i, shape):
        return 1.0 + 0.05 * jax.random.normal(ks[i], shape, f32)

    n_idx = jnp.arange(P, dtype=f32)
    return {
        "x": nrm(0, (BATCH, SEQ, D), 1.0),
        "mem": nrm(1, (BATCH, N_MEM, D), 1.0),
        "norm_mix_g": gain(2, (L, D)),
        "w_in": nrm(3, (L, D, D_IN), D ** -0.5),
        "ssm_a_re": -0.5 + 0.01 * jax.random.normal(ks[4], (L, G, P), f32),
        "ssm_a_im": math.pi * n_idx + 0.01 * jax.random.normal(ks[5], (L, G, P), f32),
        "ssm_log_step": jax.random.uniform(ks[6], (L, G), f32, math.log(1e-3), math.log(1e-1)),
        "ssm_b_re": nrm(7, (L, G, P, C), (2.0 * C) ** -0.5),
        "ssm_b_im": nrm(8, (L, G, P, C), (2.0 * C) ** -0.5),
        "ssm_c_re": nrm(9, (L, G, C, P), (2.0 * P) ** -0.5),
        "ssm_c_im": nrm(10, (L, G, C, P), (2.0 * P) ** -0.5),
        "ssm_d": nrm(11, (L, D_SSM), 1.0),
        "ssm_w_glu": nrm(12, (L, D_SSM, D_SSM), D_SSM ** -0.5),
        "ssm_b_glu": nrm(13, (L, D_SSM), 0.01),
        "ssm_out_g": gain(14, (L, D_SSM)),
        "att_lq1": nrm(15, (L, ATT_HEAD_DIM), 0.1),
        "att_lk1": nrm(16, (L, ATT_HEAD_DIM), 0.1),
        "att_lq2": nrm(17, (L, ATT_HEAD_DIM), 0.1),
        "att_lk2": nrm(18, (L, ATT_HEAD_DIM), 0.1),
        "att_subln_g": gain(19, (L, ATT_V_DIM)),
        "w_out": nrm(20, (L, D_MIX, D), D_MIX ** -0.5),
        "norm_mem_g": gain(21, (L, D)),
        "mem_norm_g": gain(22, (L, D)),
        "mem_w_q": nrm(23, (L, D, D), D ** -0.5),
        "mem_w_k": nrm(24, (L, D, D), D ** -0.5),
        "mem_w_v": nrm(25, (L, D, D), D ** -0.5),
        "mem_w_o": nrm(26, (L, D, D), D ** -0.5),
        "norm_ffn_g": gain(27, (L, D)),
        "peer_w_q": nrm(28, (L, D, PEER_HEADS * PEER_D_QUERY), D ** -0.5),
        "peer_k1": nrm(29, (L, PEER_HEADS, PEER_N_KEYS, PEER_D_HALF), PEER_D_HALF ** -0.5),
        "peer_k2": nrm(30, (L, PEER_HEADS, PEER_N_KEYS, PEER_D_HALF), PEER_D_HALF ** -0.5),
        "peer_u": nrm(31, (L, PEER_N_EXPERTS, D), D ** -0.5),
        "peer_v": nrm(32, (L, PEER_N_EXPERTS, D), PEER_HEADS ** -0.5),
        "final_g": gain(33, (D,)),
    }


def reference(x, mem, norm_mix_g, w_in, ssm_a_re, ssm_a_im, ssm_log_step, ssm_b_re, ssm_b_im,
              ssm_c_re, ssm_c_im, ssm_d, ssm_w_glu, ssm_b_glu, ssm_out_g,
              att_lq1, att_lk1, att_lq2, att_lk2, att_subln_g, w_out,
              norm_mem_g, mem_norm_g, mem_w_q, mem_w_k, mem_w_v, mem_w_o,
              norm_ffn_g, peer_w_q, peer_k1, peer_k2, peer_u, peer_v, final_g):
    bsz, s, _ = x.shape
    for l in range(DEPTH):
        n = rmsnorm(x, norm_mix_g[l])
        proj = n @ w_in[l]
        u_ssm = proj[..., :D_SSM]
        q = proj[..., D_SSM:D_SSM + D_ATT].reshape(bsz, s, N_ATT_HEADS, 2, ATT_HEAD_DIM)
        k = proj[..., D_SSM + D_ATT:D_SSM + 2 * D_ATT].reshape(bsz, s, N_ATT_HEADS, 2, ATT_HEAD_DIM)
        v = proj[..., D_SSM + 2 * D_ATT:].reshape(bsz, s, N_ATT_HEADS, ATT_V_DIM)

        y_ssm = s5_mixer(u_ssm, ssm_a_re[l], ssm_a_im[l], ssm_log_step[l], ssm_b_re[l], ssm_b_im[l],
                         ssm_c_re[l], ssm_c_im[l], ssm_d[l], ssm_w_glu[l], ssm_b_glu[l])
        y_ssm = rmsnorm(y_ssm, ssm_out_g[l])

        lam_init = 0.8 - 0.6 * math.exp(-0.3 * l)
        lam = (jnp.exp(jnp.sum(att_lq1[l].astype(jnp.float32) * att_lk1[l].astype(jnp.float32)))
               - jnp.exp(jnp.sum(att_lq2[l].astype(jnp.float32) * att_lk2[l].astype(jnp.float32)))
               + lam_init)
        y_att = diff_attention(q, k, v, lam, lam_init, att_subln_g[l])

        x = x + jnp.concatenate([y_ssm, y_att], axis=-1) @ w_out[l]

        n = rmsnorm(x, norm_mem_g[l])
        mem_n = rmsnorm(mem, mem_norm_g[l])
        x = x + mem_cross_attention(n, mem_n, mem_w_q[l], mem_w_k[l], mem_w_v[l], mem_w_o[l])

        n = rmsnorm(x, norm_ffn_g[l])
        x = x + peer_ffn(n, peer_w_q[l], peer_k1[l], peer_k2[l], peer_u[l], peer_v[l])
    return rmsnorm(x, final_g)
```

```python
import functools
import math

import jax
import jax.numpy as jnp
from jax import lax
from jax.experimental import pallas as pl
from jax.experimental.pallas import tpu as pltpu

F32 = jnp.float32
BF16 = jnp.bfloat16

EPS = 1e-6
LANES = 128
SUBLANES = 8
VMEM_LIMIT_BYTES = 56 * 1024 * 1024

CHUNK = 64
SSM_GROUP = 16
SSM_STATE = 64
SSM_BLOCK = 32
ATT_HEAD_DIM = 64
ATT_V_DIM = 2 * ATT_HEAD_DIM
N_MEM_HEADS = 4
PEER_HEADS = 8
PEER_N_KEYS = 128
PEER_D_HALF = 128
PEER_TOPK = 16

_NT = (((1,), (1,)), ((), ()))


def _params(*sem):
    return pltpu.CompilerParams(dimension_semantics=sem, vmem_limit_bytes=VMEM_LIMIT_BYTES)


def _rms(x, g):
    return x * lax.rsqrt(jnp.mean(x * x, axis=-1, keepdims=True) + EPS) * g


def _norm_matmul_body(x_ref, g_ref, w_ref, o_ref, xn_ref):
    @pl.when(pl.program_id(1) == 0)
    def _():
        xn_ref[...] = _rms(x_ref[...], g_ref[...]).astype(xn_ref.dtype)

    o_ref[...] = jnp.dot(xn_ref[...], w_ref[...], preferred_element_type=F32).astype(o_ref.dtype)


def norm_matmul(x, g, w, *, tm, tn, return_norm=False):
    m, k = x.shape
    n = w.shape[1]
    out_specs = [pl.BlockSpec((tm, tn), lambda i, j: (i, j))]
    out_shape = [jax.ShapeDtypeStruct((m, n), BF16)]
    scratch = [pltpu.VMEM((tm, k), BF16)]
    if return_norm:
        out_specs.append(pl.BlockSpec((tm, k), lambda i, j: (i, 0)))
        out_shape.append(jax.ShapeDtypeStruct((m, k), BF16))
        scratch = []
    res = pl.pallas_call(
        _norm_matmul_body,
        grid=(m // tm, n // tn),
        in_specs=[
            pl.BlockSpec((tm, k), lambda i, j: (i, 0)),
            pl.BlockSpec((1, k), lambda i, j: (0, 0)),
            pl.BlockSpec((k, tn), lambda i, j: (0, j)),
        ],
        out_specs=out_specs,
        out_shape=out_shape,
        scratch_shapes=scratch,
        compiler_params=_params("parallel", "arbitrary"),
        name="norm_matmul",
    )(x, g.reshape(1, k).astype(F32), w)
    return res if return_norm else res[0]


def _matmul_residual_body(*refs, n_terms):
    x_ref, o_ref = refs[2 * n_terms], refs[2 * n_terms + 1]
    acc = x_ref[...]
    for y_ref, w_ref in zip(refs[:n_terms], refs[n_terms:2 * n_terms]):
        acc = acc + jnp.dot(y_ref[...], w_ref[...], preferred_element_type=F32)
    o_ref[...] = acc


def matmul_residual(ys, ws, x, *, tm, tn):
    m, n = x.shape
    in_specs = [pl.BlockSpec((tm, y.shape[1]), lambda i, j: (i, 0)) for y in ys]
    in_specs += [pl.BlockSpec((w.shape[0], tn), lambda i, j: (0, j)) for w in ws]
    in_specs += [pl.BlockSpec((tm, tn), lambda i, j: (i, j))]
    return pl.pallas_call(
        functools.partial(_matmul_residual_body, n_terms=len(ys)),
        grid=(m // tm, n // tn),
        in_specs=in_specs,
        out_specs=pl.BlockSpec((tm, tn), lambda i, j: (i, j)),
        out_shape=jax.ShapeDtypeStruct((m, n), F32),
        compiler_params=_params("parallel", "parallel"),
        name="matmul_residual",
    )(*ys, *ws, x)


def _final_norm_body(x_ref, g_ref, o_ref):
    o_ref[...] = _rms(x_ref[...], g_ref[...])


def final_norm(x, g, *, tm):
    m, k = x.shape
    return pl.pallas_call(
        _final_norm_body,
        grid=(m // tm,),
        in_specs=[pl.BlockSpec((tm, k), lambda i: (i, 0)), pl.BlockSpec((1, k), lambda i: (0, 0))],
        out_specs=pl.BlockSpec((tm, k), lambda i: (i, 0)),
        out_shape=jax.ShapeDtypeStruct((m, k), F32),
        compiler_params=_params("parallel"),
        name="final_norm",
    )(x, g.reshape(1, k).astype(F32))


def _s5_kernel_matrix_body(x_ref, y_ref, o_ref, *, block):
    m = jnp.dot(x_ref[0], y_ref[0], preferred_element_type=F32, precision=lax.Precision.HIGHEST)
    rows = lax.broadcasted_iota(jnp.int32, m.shape, 0) // SSM_GROUP
    cols = lax.broadcasted_iota(jnp.int32, m.shape, 1) // SSM_GROUP
    o_ref[0] = jnp.where(rows <= cols, m, 0.0).astype(o_ref.dtype)


def s5_kernel_matrix(yt, xt):
    g, lc, p2 = yt.shape
    return pl.pallas_call(
        functools.partial(_s5_kernel_matrix_body, block=SSM_BLOCK),
        grid=(g,),
        in_specs=[pl.BlockSpec((1, lc, p2), lambda i: (i, 0, 0)), pl.BlockSpec((1, p2, lc), lambda i: (i, 0, 0))],
        out_specs=pl.BlockSpec((1, lc, lc), lambda i: (i, 0, 0)),
        out_shape=jax.ShapeDtypeStruct((g, lc, lc), BF16),
        compiler_params=_params("parallel"),
        name="s5_kernel_matrix",
    )(yt, xt)


def _s5_block_input_body(u_ref, bx_ref, z_ref):
    z_ref[0] = jnp.dot(u_ref[0], bx_ref[0], preferred_element_type=F32)


def s5_block_input(ug, bx):
    g, nb, lc = ug.shape
    p2 = bx.shape[2]
    return pl.pallas_call(
        _s5_block_input_body,
        grid=(g,),
        in_specs=[pl.BlockSpec((1, nb, lc), lambda i: (i, 0, 0)), pl.BlockSpec((1, lc, p2), lambda i: (i, 0, 0))],
        out_specs=pl.BlockSpec((1, nb, p2), lambda i: (i, 0, 0)),
        out_shape=jax.ShapeDtypeStruct((g, nb, p2), F32),
        compiler_params=_params("parallel"),
        name="s5_block_input",
    )(ug, bx)


def _s5_scan_body(z_ref, a1_ref, a2_ref, h_ref, state_ref, *, half):
    @pl.when(pl.program_id(0) == 0)
    def _():
        state_ref[...] = jnp.zeros(state_ref.shape, F32)

    a1 = a1_ref[...]
    a2 = a2_ref[...]

    def step(k, state):
        h_ref[k] = state
        return state * a1 + pltpu.roll(state, half, 1) * a2 + z_ref[k]

    state_ref[...] = lax.fori_loop(0, z_ref.shape[0], step, state_ref[...])


def s5_scan(zt, a1, a2, *, tb):
    nb, g, p2 = zt.shape
    return pl.pallas_call(
        functools.partial(_s5_scan_body, half=p2 // 2),
        grid=(nb // tb,),
        in_specs=[
            pl.BlockSpec((tb, g, p2), lambda i: (i, 0, 0)),
            pl.BlockSpec((g, p2), lambda i: (0, 0)),
            pl.BlockSpec((g, p2), lambda i: (0, 0)),
        ],
        out_specs=pl.BlockSpec((tb, g, p2), lambda i: (i, 0, 0)),
        out_shape=jax.ShapeDtypeStruct((nb, g, p2), F32),
        scratch_shapes=[pltpu.VMEM((g, p2), F32)],
        compiler_params=_params("arbitrary"),
        name="s5_scan",
    )(zt, a1, a2)


def _s5_output_body(u_ref, mt_ref, h_ref, cx_ref, d_ref, y_ref):
    u = u_ref[0]
    y = jnp.dot(u, mt_ref[0], preferred_element_type=F32)
    y = y + jnp.dot(h_ref[0], cx_ref[0], preferred_element_type=F32)
    y = y + d_ref[0] * u.astype(F32)
    y_ref[0] = jax.nn.gelu(y).astype(y_ref.dtype)


def s5_output(ug, mt, hg, cx, dg):
    g, nb, lc = ug.shape
    p2 = hg.shape[2]
    return pl.pallas_call(
        _s5_output_body,
        grid=(g,),
        in_specs=[
            pl.BlockSpec((1, nb, lc), lambda i: (i, 0, 0)),
            pl.BlockSpec((1, lc, lc), lambda i: (i, 0, 0)),
            pl.BlockSpec((1, nb, p2), lambda i: (i, 0, 0)),
            pl.BlockSpec((1, p2, lc), lambda i: (i, 0, 0)),
            pl.BlockSpec((1, 1, lc), lambda i: (i, 0, 0)),
        ],
        out_specs=pl.BlockSpec((1, nb, lc), lambda i: (i, 0, 0)),
        out_shape=jax.ShapeDtypeStruct((g, nb, lc), BF16),
        compiler_params=_params("parallel"),
        name="s5_output",
    )(ug, mt, hg, cx, dg)


def _glu_norm_body(y_ref, w_ref, b_ref, g_ref, o_ref):
    y = y_ref[...]
    z = jnp.dot(y, w_ref[...], preferred_element_type=F32) + b_ref[...]
    o = y.astype(F32) * jax.nn.sigmoid(z)
    o_ref[...] = _rms(o, g_ref[...]).astype(o_ref.dtype)


def glu_norm(y, w, b, g, *, tm):
    m, k = y.shape
    return pl.pallas_call(
        _glu_norm_body,
        grid=(m // tm,),
        in_specs=[
            pl.BlockSpec((tm, k), lambda i: (i, 0)),
            pl.BlockSpec((k, k), lambda i: (0, 0)),
            pl.BlockSpec((1, k), lambda i: (0, 0)),
            pl.BlockSpec((1, k), lambda i: (0, 0)),
        ],
        out_specs=pl.BlockSpec((tm, k), lambda i: (i, 0)),
        out_shape=jax.ShapeDtypeStruct((m, k), BF16),
        compiler_params=_params("parallel"),
        name="glu_norm",
    )(y, w, b.reshape(1, k).astype(F32), g.reshape(1, k).astype(F32))


def s5_group(u, a_re, a_im, log_step, b_re, b_im, c_re, c_im, d_skip, w_glu, b_glu, out_g):
    s, d_ssm = u.shape
    g, p = a_re.shape
    c = SSM_GROUP
    blk = SSM_BLOCK
    nb = s // blk
    lc = blk * c

    a = lax.complex(a_re.astype(F32), a_im.astype(F32))
    step = jnp.exp(log_step.astype(F32))[:, None]
    log_abar = a * step
    a_bar = jnp.exp(log_abar)
    b_bar = ((a_bar - 1.0) / a)[..., None] * lax.complex(b_re.astype(F32), b_im.astype(F32))
    cm = lax.complex(c_re.astype(F32), c_im.astype(F32))
    pos = jnp.arange(blk, dtype=F32)

    def apow(e):
        return jnp.exp(log_abar[:, None, :] * e[None, :, None])

    xm = (apow(pos)[:, :, None, :] * cm[:, None, :, :]).reshape(g, lc, p)
    ym = (jnp.swapaxes(apow(-pos), 1, 2)[:, :, :, None] * b_bar[:, :, None, :]).reshape(g, p, lc)
    x_ri = jnp.concatenate([jnp.real(xm), -jnp.imag(xm)], axis=2)
    y_ri = jnp.concatenate([jnp.real(ym), jnp.imag(ym)], axis=1)
    mt = s5_kernel_matrix(jnp.swapaxes(y_ri, 1, 2), jnp.swapaxes(x_ri, 1, 2))

    bxm = (apow(blk - 1.0 - pos)[:, :, None, :] * jnp.swapaxes(b_bar, 1, 2)[:, None, :, :]).reshape(g, lc, p)
    bx = jnp.concatenate([jnp.real(bxm), jnp.imag(bxm)], axis=2).astype(BF16)
    cxm = jnp.swapaxes((apow(pos + 1.0)[:, :, None, :] * cm[:, None, :, :]).reshape(g, lc, p), 1, 2)
    cx = jnp.concatenate([jnp.real(cxm), -jnp.imag(cxm)], axis=1).astype(BF16)
    a_blk = jnp.exp(log_abar * blk)
    a1 = jnp.concatenate([jnp.real(a_blk), jnp.real(a_blk)], axis=1)
    a2 = jnp.concatenate([-jnp.imag(a_blk), jnp.imag(a_blk)], axis=1)
    dg = jnp.tile(d_skip.astype(F32).reshape(g, 1, c), (1, blk, 1)).reshape(g, 1, lc)

    ug = u.reshape(nb, blk, g, c).transpose(2, 0, 1, 3).reshape(g, nb, lc)
    z = s5_block_input(ug, bx)
    h = s5_scan(z.transpose(1, 0, 2), a1, a2, tb=min(64, nb))
    hg = h.transpose(1, 0, 2).astype(BF16)
    yg = s5_output(ug, mt, hg, cx, dg)
    y = yg.reshape(g, nb, blk, c).transpose(1, 2, 0, 3).reshape(s, d_ssm)
    return glu_norm(y, w_glu.astype(BF16), b_glu, out_g, tm=min(512, s))


def _diff_attn_body(lam_ref, q_ref, k_ref, v_ref, g_ref, o_ref, m_scr, l_scr, acc_scr, *, tq, out_scale):
    i = pl.program_id(1)
    q = q_ref[...] * jnp.asarray(ATT_HEAD_DIM ** -0.5, q_ref.dtype)
    lane = lax.broadcasted_iota(jnp.int32, q.shape, 1)
    zero = jnp.zeros_like(q)
    qq = jnp.concatenate([jnp.where(lane < ATT_HEAD_DIM, q, zero), jnp.where(lane >= ATT_HEAD_DIM, q, zero)], axis=0)

    m_scr[...] = jnp.full(m_scr.shape, -jnp.inf, F32)
    l_scr[...] = jnp.zeros(l_scr.shape, F32)
    acc_scr[...] = jnp.zeros(acc_scr.shape, F32)
    reps = tq // LANES

    def step(start, masked):
        kb = k_ref[pl.ds(start, tq), :]
        vb = v_ref[pl.ds(start, tq), :]
        s = lax.dot_general(qq, kb, _NT, preferred_element_type=F32)
        if masked:
            qc = (lax.broadcasted_iota(jnp.int32, s.shape, 0) % tq) // CHUNK
            kc = lax.broadcasted_iota(jnp.int32, s.shape, 1) // CHUNK
            s = jnp.where(kc <= qc, s, -jnp.inf)
        m_prev = m_scr[...]
        m_next = jnp.maximum(m_prev, jnp.max(s, axis=1, keepdims=True))
        alpha = jnp.exp(m_prev - m_next)
        p = jnp.exp(s - jnp.concatenate([m_next] * reps, axis=1))
        l_scr[...] = alpha * l_scr[...] + jnp.sum(p, axis=1, keepdims=True)
        acc_scr[...] = alpha * acc_scr[...] + jnp.dot(p.astype(vb.dtype), vb, preferred_element_type=F32)
        m_scr[...] = m_next

    def full_step(j, carry):
        step(pl.multiple_of(j * tq, tq), False)
        return carry

    lax.fori_loop(0, i, full_step, 0)
    step(pl.multiple_of(i * tq, tq), True)

    o = acc_scr[...] / l_scr[...]
    d = o[:tq] - lam_ref[0] * o[tq:]
    o_ref[...] = (_rms(d, g_ref[...]) * out_scale).astype(o_ref.dtype)


def diff_attention(proj, lam, subln_g, *, n_heads, q_col, k_col, v_col, lam_init, tq):
    s = proj.shape[0]
    dv = ATT_V_DIM
    return pl.pallas_call(
        functools.partial(_diff_attn_body, tq=tq, out_scale=1.0 - lam_init),
        grid=(n_heads, s // tq),
        in_specs=[
            pl.BlockSpec(memory_space=pltpu.SMEM),
            pl.BlockSpec((tq, dv), lambda h, i: (i, q_col // dv + h)),
            pl.BlockSpec((s, dv), lambda h, i: (0, k_col // dv + h)),
            pl.BlockSpec((s, dv), lambda h, i: (0, v_col // dv + h)),
            pl.BlockSpec((1, dv), lambda h, i: (0, 0)),
        ],
        out_specs=pl.BlockSpec((tq, dv), lambda h, i: (i, h)),
        out_shape=jax.ShapeDtypeStruct((s, n_heads * dv), BF16),
        scratch_shapes=[
            pltpu.VMEM((2 * tq, LANES), F32),
            pltpu.VMEM((2 * tq, LANES), F32),
            pltpu.VMEM((2 * tq, dv), F32),
        ],
        compiler_params=_params("parallel", "arbitrary"),
        name="diff_attention",
    )(lam.reshape(1).astype(F32), proj, proj, proj, subln_g.reshape(1, dv).astype(F32))


def _mem_attn_body(q_ref, k_ref, v_ref, o_ref, *, scale):
    s = lax.dot_general(q_ref[...], k_ref[...], _NT, preferred_element_type=F32) * scale
    p = jnp.exp(s - jnp.max(s, axis=1, keepdims=True))
    p = p / jnp.sum(p, axis=1, keepdims=True)
    o_ref[...] = jnp.dot(p.astype(v_ref.dtype), v_ref[...], preferred_element_type=F32).astype(o_ref.dtype)


def mem_attention(q, kv, *, n_heads, tm):
    s, d = q.shape
    m = kv.shape[0]
    dh = d // n_heads
    return pl.pallas_call(
        functools.partial(_mem_attn_body, scale=dh ** -0.5),
        grid=(s // tm, n_heads),
        in_specs=[
            pl.BlockSpec((tm, dh), lambda i, h: (i, h)),
            pl.BlockSpec((m, dh), lambda i, h: (0, h)),
            pl.BlockSpec((m, dh), lambda i, h: (0, n_heads + h)),
        ],
        out_specs=pl.BlockSpec((tm, dh), lambda i, h: (i, h)),
        out_shape=jax.ShapeDtypeStruct((s, d), BF16),
        compiler_params=_params("parallel", "parallel"),
        name="mem_attention",
    )(q, kv, kv)


def _cmpx(xs, i, j):
    a, b = xs[i], xs[j]
    if b is None:
        return
    if a is None:
        xs[i], xs[j] = b, None
        return
    xs[i], xs[j] = jnp.maximum(a, b), jnp.minimum(a, b)


def _bitonic_merge_desc(xs):
    n = len(xs)
    j = n // 2
    while j >= 1:
        for i in range(n):
            if (i ^ j) > i:
                _cmpx(xs, i, i ^ j)
        j //= 2
    return xs


def _sort_desc(xs):
    n = len(xs)
    k = 2
    while k <= n:
        j = k // 2
        while j >= 1:
            for i in range(n):
                l = i ^ j
                if l > i:
                    if (i & k) == 0:
                        _cmpx(xs, i, l)
                    else:
                        _cmpx(xs, l, i)
            j //= 2
        k *= 2
    return xs


def _max_or_none(a, b):
    if a is None:
        return b
    if b is None:
        return a
    return jnp.maximum(a, b)


def _merge_top(a, b):
    n = len(a)
    return _bitonic_merge_desc([_max_or_none(a[i], b[n - 1 - i]) for i in range(n)])


def _top_keys(st):
    n_keys = st.shape[0]
    k = PEER_TOPK
    parts = [st[r * SUBLANES:(r + 1) * SUBLANES, :] for r in range(n_keys // SUBLANES)]
    tops = None
    for base in range(0, len(parts), k):
        grp = _sort_desc(parts[base:base + k])
        tops = grp if tops is None else _merge_top(tops, grp)
    shift = SUBLANES // 2
    while shift >= 1:
        tops = _merge_top(tops, [pltpu.roll(x, shift, 0) for x in tops])
        shift //= 2
    return tops


def _grid_top(v1, v2):
    k = PEER_TOPK
    rows = [[v1[i] + v2[j] for j in range(k // (i + 1))] for i in range(k)]
    pad = lambda xs: xs + [None] * (k - len(xs))
    tops = _merge_top(rows[0], pad(rows[1]))
    rest = [x for r in rows[2:] for x in r]
    for base in range(0, len(rest), k):
        tops = _merge_top(tops, _sort_desc(pad(rest[base:base + k])))
    return tops


def _peer_route_body(q_ref, k1_ref, k2_ref, d1_ref, s2_ref, e1_ref, e2_ref):
    n_keys = k1_ref.shape[1]
    dq = k1_ref.shape[2]
    for h in range(k1_ref.shape[0]):
        q1 = q_ref[:, (2 * h) * dq:(2 * h + 1) * dq]
        q2 = q_ref[:, (2 * h + 1) * dq:(2 * h + 2) * dq]
        s1 = lax.dot_general(k1_ref[h], q1, _NT, preferred_element_type=F32)
        s2 = lax.dot_general(k2_ref[h], q2, _NT, preferred_element_type=F32)
        v1 = _top_keys(s1)
        v2 = _top_keys(s2)
        top = _grid_top(v1, v2)
        z = None
        for c in top:
            e = jnp.exp(c - top[0])
            z = e if z is None else z + e
        bc = lambda x: jnp.broadcast_to(x[0:1, :], (n_keys, x.shape[1]))
        d1_ref[h] = bc(top[PEER_TOPK - 1]) - s1
        s2_ref[h] = s2
        e1_ref[h] = jnp.exp(s1 - bc(v1[0]))
        e2_ref[h] = jnp.exp(s2 - bc(v2[0])) / bc(z)


def peer_route(q, k1, k2, *, tt):
    s, dqt = q.shape
    h, n_keys, dq = k1.shape
    out = jax.ShapeDtypeStruct((h, n_keys, s), F32)
    spec = pl.BlockSpec((h, n_keys, tt), lambda i: (0, 0, i))
    kspec = pl.BlockSpec((h, n_keys, dq), lambda i: (0, 0, 0))
    return pl.pallas_call(
        _peer_route_body,
        grid=(s // tt,),
        in_specs=[pl.BlockSpec((tt, dqt), lambda i: (i, 0)), kspec, kspec],
        out_specs=[spec] * 4,
        out_shape=[out] * 4,
        compiler_params=_params("parallel"),
        name="peer_route",
    )(q, k1, k2)


def _peer_mix_body(n_ref, u_ref, vt_ref, d1_ref, s2_ref, e1_ref, e2_ref, x_ref, o_ref, acc_ref, h_ref):
    j = pl.program_id(1)
    te, tt = h_ref.shape
    n_keys = s2_ref.shape[1]

    @pl.when(j == 0)
    def _():
        acc_ref[...] = jnp.zeros(acc_ref.shape, F32)

    at = lax.dot_general(u_ref[...], n_ref[...], _NT, preferred_element_type=F32)
    for r in range(te // n_keys):
        for c in range(tt // LANES):
            cols = slice(c * LANES, (c + 1) * LANES)
            w = jnp.zeros((n_keys, LANES), F32)
            for h in range(s2_ref.shape[0]):
                d1 = d1_ref[0, h, r:r + 1, cols]
                e1v = e1_ref[0, h, r:r + 1, cols]
                w = w + jnp.where(s2_ref[h, :, cols] >= d1, e2_ref[h, :, cols] * e1v, 0.0)
            a = at[r * n_keys:(r + 1) * n_keys, cols]
            h_ref[r * n_keys:(r + 1) * n_keys, cols] = (jax.nn.gelu(a) * w).astype(h_ref.dtype)
    acc_ref[...] += jnp.dot(vt_ref[...], h_ref[...], preferred_element_type=F32)

    @pl.when(j == pl.num_programs(1) - 1)
    def _():
        o_ref[...] = x_ref[...] + acc_ref[...].T


def peer_mix(n, u_tab, vt_tab, route, x, *, tt, te):
    s, d = n.shape
    e = u_tab.shape[0]
    d1, s2, e1, e2 = route
    h, n_keys, _ = s2.shape
    rows = te // n_keys
    regroup = lambda a: a.reshape(h, n_keys // rows, rows, s).transpose(1, 0, 2, 3)
    route = (regroup(d1), s2, regroup(e1), e2)
    rspec = pl.BlockSpec((h, n_keys, tt), lambda i, j: (0, 0, i))
    gspec = pl.BlockSpec((1, h, rows, tt), lambda i, j: (j, 0, 0, i))
    return pl.pallas_call(
        _peer_mix_body,
        grid=(s // tt, e // te),
        in_specs=[
            pl.BlockSpec((tt, d), lambda i, j: (i, 0)),
            pl.BlockSpec((te, d), lambda i, j: (j, 0)),
            pl.BlockSpec((d, te), lambda i, j: (0, j)),
            gspec, rspec, gspec, rspec,
            pl.BlockSpec((tt, d), lambda i, j: (i, 0)),
        ],
        out_specs=pl.BlockSpec((tt, d), lambda i, j: (i, 0)),
        out_shape=jax.ShapeDtypeStruct((s, d), F32),
        scratch_shapes=[pltpu.VMEM((d, tt), F32), pltpu.VMEM((te, tt), BF16)],
        compiler_params=_params("parallel", "arbitrary"),
        name="peer_mix",
    )(n, u_tab, vt_tab, *route, x)


def kernel(x, mem, norm_mix_g, w_in, ssm_a_re, ssm_a_im, ssm_log_step, ssm_b_re, ssm_b_im, ssm_c_re, ssm_c_im, ssm_d, ssm_w_glu, ssm_b_glu, ssm_out_g, att_lq1, att_lk1, att_lq2, att_lk2, att_subln_g, w_out, norm_mem_g, mem_norm_g, mem_w_q, mem_w_k, mem_w_v, mem_w_o, norm_ffn_g, peer_w_q, peer_k1, peer_k2, peer_u, peer_v, final_g):
    bsz, s, d = x.shape
    assert bsz == 1
    depth = w_in.shape[0]
    d_ssm = ssm_d.shape[1]
    d_att = (w_in.shape[2] - d_ssm) // 3
    n_att_heads = d_att // ATT_V_DIM
    tm = min(512, s)

    xs = x.reshape(s, d).astype(F32)
    mem2 = mem.reshape(mem.shape[1], d).astype(F32)
    for l in range(depth):
        proj = norm_matmul(xs, norm_mix_g[l], w_in[l].astype(BF16), tm=tm, tn=1024)
        y_ssm = s5_group(proj[:, :d_ssm], ssm_a_re[l], ssm_a_im[l], ssm_log_step[l], ssm_b_re[l], ssm_b_im[l],
                         ssm_c_re[l], ssm_c_im[l], ssm_d[l], ssm_w_glu[l], ssm_b_glu[l], ssm_out_g[l])
        lam_init = 0.8 - 0.6 * math.exp(-0.3 * l)
        lam = (jnp.exp(jnp.sum(att_lq1[l].astype(F32) * att_lk1[l].astype(F32)))
               - jnp.exp(jnp.sum(att_lq2[l].astype(F32) * att_lk2[l].astype(F32))) + lam_init)
        y_att = diff_attention(proj, lam, att_subln_g[l], n_heads=n_att_heads, q_col=d_ssm, k_col=d_ssm + d_att,
                               v_col=d_ssm + 2 * d_att, lam_init=lam_init, tq=tm)
        w_o = w_out[l].astype(BF16)
        xs = matmul_residual([y_ssm, y_att], [w_o[:d_ssm], w_o[d_ssm:]], xs, tm=tm, tn=1024)

        q = norm_matmul(xs, norm_mem_g[l], mem_w_q[l].astype(BF16), tm=tm, tn=1024)
        w_kv = jnp.concatenate([mem_w_k[l], mem_w_v[l]], axis=1).astype(BF16)
        kv = norm_matmul(mem2, mem_norm_g[l], w_kv, tm=mem2.shape[0], tn=1024)
        o = mem_attention(q, kv, n_heads=N_MEM_HEADS, tm=tm)
        xs = matmul_residual([o], [mem_w_o[l].astype(BF16)], xs, tm=tm, tn=1024)

        pq, n = norm_matmul(xs, norm_ffn_g[l], peer_w_q[l].astype(BF16), tm=tm, tn=1024, return_norm=True)
        route = peer_route(pq, peer_k1[l].astype(BF16), peer_k2[l].astype(BF16), tt=min(256, s))
        xs = peer_mix(n, peer_u[l].astype(BF16), peer_v[l].astype(BF16).T, route, xs, tt=tm, te=512)
    return final_norm(xs, final_g, tm=tm).reshape(bsz, s, d)
```

```python
import functools
import math

import jax
import jax.numpy as jnp
from jax import lax
from jax.experimental import pallas as pl
from jax.experimental.pallas import tpu as pltpu

F32 = jnp.float32
BF16 = jnp.bfloat16

EPS = 1e-6
LANES = 128
SUBLANES = 8
MXU_DIM = 256
VMEM_LIMIT_BYTES = 56 * 1024 * 1024

CHUNK = 64
SSM_GROUP = 16
SSM_STATE = 64
SSM_BLOCK = 32
ATT_HEAD_DIM = 64
ATT_V_DIM = 2 * ATT_HEAD_DIM
N_MEM_HEADS = 4
PEER_HEADS = 8
PEER_N_KEYS = 128
PEER_D_HALF = 128
PEER_TOPK = 16
SLAB_KEYS = 32

_NT = (((1,), (1,)), ((), ()))


def _params(*sem):
    return pltpu.CompilerParams(dimension_semantics=sem, vmem_limit_bytes=VMEM_LIMIT_BYTES)


def _rms(x, g):
    return x * lax.rsqrt(jnp.mean(x * x, axis=-1, keepdims=True) + EPS) * g


def _norm_matmul_body(x_ref, g_ref, w_ref, o_ref, xn_ref):
    @pl.when(pl.program_id(1) == 0)
    def _():
        xn_ref[...] = _rms(x_ref[...], g_ref[...]).astype(xn_ref.dtype)

    o_ref[...] = jnp.dot(xn_ref[...], w_ref[...], preferred_element_type=F32).astype(o_ref.dtype)


def norm_matmul(x, g, w, *, tm, tn, return_norm=False):
    m, k = x.shape
    n = w.shape[1]
    out_specs = [pl.BlockSpec((tm, tn), lambda i, j: (i, j))]
    out_shape = [jax.ShapeDtypeStruct((m, n), BF16)]
    scratch = [pltpu.VMEM((tm, k), BF16)]
    if return_norm:
        out_specs.append(pl.BlockSpec((tm, k), lambda i, j: (i, 0)))
        out_shape.append(jax.ShapeDtypeStruct((m, k), BF16))
        scratch = []
    res = pl.pallas_call(
        _norm_matmul_body,
        grid=(m // tm, n // tn),
        in_specs=[
            pl.BlockSpec((tm, k), lambda i, j: (i, 0)),
            pl.BlockSpec((1, k), lambda i, j: (0, 0)),
            pl.BlockSpec((k, tn), lambda i, j: (0, j)),
        ],
        out_specs=out_specs,
        out_shape=out_shape,
        scratch_shapes=scratch,
        compiler_params=_params("parallel", "arbitrary"),
        name="norm_matmul",
    )(x, g.reshape(1, k).astype(F32), w)
    return res if return_norm else res[0]


def _matmul_residual_body(*refs, n_terms):
    x_ref, o_ref = refs[2 * n_terms], refs[2 * n_terms + 1]
    acc = x_ref[...]
    for y_ref, w_ref in zip(refs[:n_terms], refs[n_terms:2 * n_terms]):
        acc = acc + jnp.dot(y_ref[...], w_ref[...], preferred_element_type=F32)
    o_ref[...] = acc


def matmul_residual(ys, ws, x, *, tm, tn):
    m, n = x.shape
    in_specs = [pl.BlockSpec((tm, y.shape[1]), lambda i, j: (i, 0)) for y in ys]
    in_specs += [pl.BlockSpec((w.shape[0], tn), lambda i, j: (0, j)) for w in ws]
    in_specs += [pl.BlockSpec((tm, tn), lambda i, j: (i, j))]
    return pl.pallas_call(
        functools.partial(_matmul_residual_body, n_terms=len(ys)),
        grid=(m // tm, n // tn),
        in_specs=in_specs,
        out_specs=pl.BlockSpec((tm, tn), lambda i, j: (i, j)),
        out_shape=jax.ShapeDtypeStruct((m, n), F32),
        compiler_params=_params("parallel", "parallel"),
        name="matmul_residual",
    )(*ys, *ws, x)


def _final_norm_body(x_ref, g_ref, o_ref):
    o_ref[...] = _rms(x_ref[...], g_ref[...])


def final_norm(x, g, *, tm):
    m, k = x.shape
    return pl.pallas_call(
        _final_norm_body,
        grid=(m // tm,),
        in_specs=[pl.BlockSpec((tm, k), lambda i: (i, 0)), pl.BlockSpec((1, k), lambda i: (0, 0))],
        out_specs=pl.BlockSpec((tm, k), lambda i: (i, 0)),
        out_shape=jax.ShapeDtypeStruct((m, k), F32),
        compiler_params=_params("parallel"),
        name="final_norm",
    )(x, g.reshape(1, k).astype(F32))


def _s5_kernel_matrix_body(x_ref, y_ref, o_ref, *, block):
    m = jnp.dot(x_ref[0], y_ref[0], preferred_element_type=F32, precision=lax.Precision.HIGHEST)
    rows = lax.broadcasted_iota(jnp.int32, m.shape, 0) // SSM_GROUP
    cols = lax.broadcasted_iota(jnp.int32, m.shape, 1) // SSM_GROUP
    o_ref[0] = jnp.where(rows <= cols, m, 0.0).astype(o_ref.dtype)


def s5_kernel_matrix(yt, xt):
    g, lc, p2 = yt.shape
    return pl.pallas_call(
        functools.partial(_s5_kernel_matrix_body, block=SSM_BLOCK),
        grid=(g,),
        in_specs=[pl.BlockSpec((1, lc, p2), lambda i: (i, 0, 0)), pl.BlockSpec((1, p2, lc), lambda i: (i, 0, 0))],
        out_specs=pl.BlockSpec((1, lc, lc), lambda i: (i, 0, 0)),
        out_shape=jax.ShapeDtypeStruct((g, lc, lc), BF16),
        compiler_params=_params("parallel"),
        name="s5_kernel_matrix",
    )(yt, xt)


def _s5_block_input_body(u_ref, bx_ref, z_ref):
    z_ref[0] = jnp.dot(u_ref[0], bx_ref[0], preferred_element_type=F32)


def s5_block_input(ug, bx):
    g, nb, lc = ug.shape
    p2 = bx.shape[2]
    return pl.pallas_call(
        _s5_block_input_body,
        grid=(g,),
        in_specs=[pl.BlockSpec((1, nb, lc), lambda i: (i, 0, 0)), pl.BlockSpec((1, lc, p2), lambda i: (i, 0, 0))],
        out_specs=pl.BlockSpec((1, nb, p2), lambda i: (i, 0, 0)),
        out_shape=jax.ShapeDtypeStruct((g, nb, p2), F32),
        compiler_params=_params("parallel"),
        name="s5_block_input",
    )(ug, bx)


def _s5_scan_body(z_ref, a1_ref, a2_ref, h_ref, state_ref, *, half):
    @pl.when(pl.program_id(0) == 0)
    def _():
        state_ref[...] = jnp.zeros(state_ref.shape, F32)

    a1 = a1_ref[...]
    a2 = a2_ref[...]

    def step(k, state):
        h_ref[k] = state
        return state * a1 + pltpu.roll(state, half, 1) * a2 + z_ref[k]

    state_ref[...] = lax.fori_loop(0, z_ref.shape[0], step, state_ref[...])


def s5_scan(zt, a1, a2, *, tb):
    nb, g, p2 = zt.shape
    return pl.pallas_call(
        functools.partial(_s5_scan_body, half=p2 // 2),
        grid=(nb // tb,),
        in_specs=[
            pl.BlockSpec((tb, g, p2), lambda i: (i, 0, 0)),
            pl.BlockSpec((g, p2), lambda i: (0, 0)),
            pl.BlockSpec((g, p2), lambda i: (0, 0)),
        ],
        out_specs=pl.BlockSpec((tb, g, p2), lambda i: (i, 0, 0)),
        out_shape=jax.ShapeDtypeStruct((nb, g, p2), F32),
        scratch_shapes=[pltpu.VMEM((g, p2), F32)],
        compiler_params=_params("arbitrary"),
        name="s5_scan",
    )(zt, a1, a2)


def _s5_output_body(u_ref, mt_ref, h_ref, cx_ref, d_ref, y_ref):
    u = u_ref[0]
    y = jnp.dot(u, mt_ref[0], preferred_element_type=F32)
    y = y + jnp.dot(h_ref[0], cx_ref[0], preferred_element_type=F32)
    y = y + d_ref[0] * u.astype(F32)
    y_ref[0] = jax.nn.gelu(y).astype(y_ref.dtype)


def s5_output(ug, mt, hg, cx, dg):
    g, nb, lc = ug.shape
    p2 = hg.shape[2]
    return pl.pallas_call(
        _s5_output_body,
        grid=(g,),
        in_specs=[
            pl.BlockSpec((1, nb, lc), lambda i: (i, 0, 0)),
            pl.BlockSpec((1, lc, lc), lambda i: (i, 0, 0)),
            pl.BlockSpec((1, nb, p2), lambda i: (i, 0, 0)),
            pl.BlockSpec((1, p2, lc), lambda i: (i, 0, 0)),
            pl.BlockSpec((1, 1, lc), lambda i: (i, 0, 0)),
        ],
        out_specs=pl.BlockSpec((1, nb, lc), lambda i: (i, 0, 0)),
        out_shape=jax.ShapeDtypeStruct((g, nb, lc), BF16),
        compiler_params=_params("parallel"),
        name="s5_output",
    )(ug, mt, hg, cx, dg)


def _glu_norm_body(y_ref, w_ref, b_ref, g_ref, o_ref):
    y = y_ref[...]
    z = jnp.dot(y, w_ref[...], preferred_element_type=F32) + b_ref[...]
    o = y.astype(F32) * jax.nn.sigmoid(z)
    o_ref[...] = _rms(o, g_ref[...]).astype(o_ref.dtype)


def glu_norm(y, w, b, g, *, tm):
    m, k = y.shape
    return pl.pallas_call(
        _glu_norm_body,
        grid=(m // tm,),
        in_specs=[
            pl.BlockSpec((tm, k), lambda i: (i, 0)),
            pl.BlockSpec((k, k), lambda i: (0, 0)),
            pl.BlockSpec((1, k), lambda i: (0, 0)),
            pl.BlockSpec((1, k), lambda i: (0, 0)),
        ],
        out_specs=pl.BlockSpec((tm, k), lambda i: (i, 0)),
        out_shape=jax.ShapeDtypeStruct((m, k), BF16),
        compiler_params=_params("parallel"),
        name="glu_norm",
    )(y, w, b.reshape(1, k).astype(F32), g.reshape(1, k).astype(F32))


def s5_group(u, a_re, a_im, log_step, b_re, b_im, c_re, c_im, d_skip, w_glu, b_glu, out_g):
    s, d_ssm = u.shape
    g, p = a_re.shape
    c = SSM_GROUP
    blk = SSM_BLOCK
    nb = s // blk
    lc = blk * c

    a = lax.complex(a_re.astype(F32), a_im.astype(F32))
    step = jnp.exp(log_step.astype(F32))[:, None]
    log_abar = a * step
    a_bar = jnp.exp(log_abar)
    b_bar = ((a_bar - 1.0) / a)[..., None] * lax.complex(b_re.astype(F32), b_im.astype(F32))
    cm = lax.complex(c_re.astype(F32), c_im.astype(F32))
    pos = jnp.arange(blk, dtype=F32)

    def apow(e):
        return jnp.exp(log_abar[:, None, :] * e[None, :, None])

    xm = (apow(pos)[:, :, None, :] * cm[:, None, :, :]).reshape(g, lc, p)
    ym = (jnp.swapaxes(apow(-pos), 1, 2)[:, :, :, None] * b_bar[:, :, None, :]).reshape(g, p, lc)
    x_ri = jnp.concatenate([jnp.real(xm), -jnp.imag(xm)], axis=2)
    y_ri = jnp.concatenate([jnp.real(ym), jnp.imag(ym)], axis=1)
    mt = s5_kernel_matrix(jnp.swapaxes(y_ri, 1, 2), jnp.swapaxes(x_ri, 1, 2))

    bxm = (apow(blk - 1.0 - pos)[:, :, None, :] * jnp.swapaxes(b_bar, 1, 2)[:, None, :, :]).reshape(g, lc, p)
    bx = jnp.concatenate([jnp.real(bxm), jnp.imag(bxm)], axis=2).astype(BF16)
    cxm = jnp.swapaxes((apow(pos + 1.0)[:, :, None, :] * cm[:, None, :, :]).reshape(g, lc, p), 1, 2)
    cx = jnp.concatenate([jnp.real(cxm), -jnp.imag(cxm)], axis=1).astype(BF16)
    a_blk = jnp.exp(log_abar * blk)
    a1 = jnp.concatenate([jnp.real(a_blk), jnp.real(a_blk)], axis=1)
    a2 = jnp.concatenate([-jnp.imag(a_blk), jnp.imag(a_blk)], axis=1)
    dg = jnp.tile(d_skip.astype(F32).reshape(g, 1, c), (1, blk, 1)).reshape(g, 1, lc)

    ug = u.reshape(nb, blk, g, c).transpose(2, 0, 1, 3).reshape(g, nb, lc)
    z = s5_block_input(ug, bx)
    h = s5_scan(z.transpose(1, 0, 2), a1, a2, tb=min(64, nb))
    hg = h.transpose(1, 0, 2).astype(BF16)
    yg = s5_output(ug, mt, hg, cx, dg)
    y = yg.reshape(g, nb, blk, c).transpose(1, 2, 0, 3).reshape(s, d_ssm)
    return glu_norm(y, w_glu.astype(BF16), b_glu, out_g, tm=min(512, s))


def _diff_attn_body(lam_ref, q_ref, k_ref, v_ref, g_ref, o_ref, m_scr, l_scr, acc_scr, *, tq, out_scale):
    i = pl.program_id(1)
    q = q_ref[...] * jnp.asarray(ATT_HEAD_DIM ** -0.5, q_ref.dtype)
    lane = lax.broadcasted_iota(jnp.int32, q.shape, 1)
    zero = jnp.zeros_like(q)
    qq = jnp.concatenate([jnp.where(lane < ATT_HEAD_DIM, q, zero), jnp.where(lane >= ATT_HEAD_DIM, q, zero)], axis=0)

    m_scr[...] = jnp.full(m_scr.shape, -jnp.inf, F32)
    l_scr[...] = jnp.zeros(l_scr.shape, F32)
    acc_scr[...] = jnp.zeros(acc_scr.shape, F32)
    reps = tq // LANES

    def step(start, masked):
        kb = k_ref[pl.ds(start, tq), :]
        vb = v_ref[pl.ds(start, tq), :]
        s = lax.dot_general(qq, kb, _NT, preferred_element_type=F32)
        if masked:
            qc = (lax.broadcasted_iota(jnp.int32, s.shape, 0) % tq) // CHUNK
            kc = lax.broadcasted_iota(jnp.int32, s.shape, 1) // CHUNK
            s = jnp.where(kc <= qc, s, -jnp.inf)
        m_prev = m_scr[...]
        m_next = jnp.maximum(m_prev, jnp.max(s, axis=1, keepdims=True))
        alpha = jnp.exp(m_prev - m_next)
        p = jnp.exp(s - jnp.concatenate([m_next] * reps, axis=1))
        l_scr[...] = alpha * l_scr[...] + jnp.sum(p, axis=1, keepdims=True)
        acc_scr[...] = alpha * acc_scr[...] + jnp.dot(p.astype(vb.dtype), vb, preferred_element_type=F32)
        m_scr[...] = m_next

    def full_pair(j, carry):
        step(pl.multiple_of(2 * j * tq, tq), False)
        step(pl.multiple_of((2 * j + 1) * tq, tq), False)
        return carry

    lax.fori_loop(0, lax.shift_right_logical(i, 1), full_pair, 0)
    diag = pl.multiple_of(i * tq, tq)

    @pl.when((i & 1) == 1)
    def _():
        step(pl.multiple_of((i - 1) * tq, tq), False)
        step(diag, True)

    @pl.when((i & 1) == 0)
    def _():
        step(diag, True)

    o = acc_scr[...] / l_scr[...]
    d = o[:tq] - lam_ref[0] * o[tq:]
    o_ref[...] = (_rms(d, g_ref[...]) * out_scale).astype(o_ref.dtype)


def diff_attention(proj, lam, subln_g, *, n_heads, q_col, k_col, v_col, lam_init, tq):
    s = proj.shape[0]
    dv = ATT_V_DIM
    return pl.pallas_call(
        functools.partial(_diff_attn_body, tq=tq, out_scale=1.0 - lam_init),
        grid=(n_heads, s // tq),
        in_specs=[
            pl.BlockSpec(memory_space=pltpu.SMEM),
            pl.BlockSpec((tq, dv), lambda h, i: (i, q_col // dv + h)),
            pl.BlockSpec((s, dv), lambda h, i: (0, k_col // dv + h)),
            pl.BlockSpec((s, dv), lambda h, i: (0, v_col // dv + h)),
            pl.BlockSpec((1, dv), lambda h, i: (0, 0)),
        ],
        out_specs=pl.BlockSpec((tq, dv), lambda h, i: (i, h)),
        out_shape=jax.ShapeDtypeStruct((s, n_heads * dv), BF16),
        scratch_shapes=[
            pltpu.VMEM((2 * tq, LANES), F32),
            pltpu.VMEM((2 * tq, LANES), F32),
            pltpu.VMEM((2 * tq, dv), F32),
        ],
        compiler_params=_params("parallel", "arbitrary"),
        name="diff_attention",
    )(lam.reshape(1).astype(F32), proj, proj, proj, subln_g.reshape(1, dv).astype(F32))


def _mem_attn_body(q_ref, k_ref, v_ref, o_ref, *, scale):
    s = lax.dot_general(q_ref[...], k_ref[...], _NT, preferred_element_type=F32) * scale
    p = jnp.exp(s - jnp.max(s, axis=1, keepdims=True))
    p = p / jnp.sum(p, axis=1, keepdims=True)
    o_ref[...] = jnp.dot(p.astype(v_ref.dtype), v_ref[...], preferred_element_type=F32).astype(o_ref.dtype)


def mem_attention(q, kv, *, n_heads, tm):
    s, d = q.shape
    m = kv.shape[0]
    dh = d // n_heads
    return pl.pallas_call(
        functools.partial(_mem_attn_body, scale=dh ** -0.5),
        grid=(s // tm, n_heads),
        in_specs=[
            pl.BlockSpec((tm, dh), lambda i, h: (i, h)),
            pl.BlockSpec((m, dh), lambda i, h: (0, h)),
            pl.BlockSpec((m, dh), lambda i, h: (0, n_heads + h)),
        ],
        out_specs=pl.BlockSpec((tm, dh), lambda i, h: (i, h)),
        out_shape=jax.ShapeDtypeStruct((s, d), BF16),
        compiler_params=_params("parallel", "parallel"),
        name="mem_attention",
    )(q, kv, kv)


def _cmpx(xs, i, j):
    a, b = xs[i], xs[j]
    if b is None:
        return
    if a is None:
        xs[i], xs[j] = b, None
        return
    xs[i], xs[j] = jnp.maximum(a, b), jnp.minimum(a, b)


def _bitonic_merge_desc(xs):
    n = len(xs)
    j = n // 2
    while j >= 1:
        for i in range(n):
            if (i ^ j) > i:
                _cmpx(xs, i, i ^ j)
        j //= 2
    return xs


def _sort_desc(xs):
    n = len(xs)
    k = 2
    while k <= n:
        j = k // 2
        while j >= 1:
            for i in range(n):
                l = i ^ j
                if l > i:
                    if (i & k) == 0:
                        _cmpx(xs, i, l)
                    else:
                        _cmpx(xs, l, i)
            j //= 2
        k *= 2
    return xs


def _max_or_none(a, b):
    if a is None:
        return b
    if b is None:
        return a
    return jnp.maximum(a, b)


def _merge_top(a, b):
    n = len(a)
    return _bitonic_merge_desc([_max_or_none(a[i], b[n - 1 - i]) for i in range(n)])


def _top_keys(st):
    n_keys = st.shape[0]
    k = PEER_TOPK
    parts = [st[r * SUBLANES:(r + 1) * SUBLANES, :] for r in range(n_keys // SUBLANES)]
    tops = None
    for base in range(0, len(parts), k):
        grp = _sort_desc(parts[base:base + k])
        tops = grp if tops is None else _merge_top(tops, grp)
    shift = SUBLANES // 2
    while shift >= 1:
        tops = _merge_top(tops, [pltpu.roll(x, shift, 0) for x in tops])
        shift //= 2
    return tops


def _grid_top(v1, v2):
    k = PEER_TOPK
    rows = [[v1[i] + v2[j] for j in range(k // (i + 1))] for i in range(k)]
    pad = lambda xs: xs + [None] * (k - len(xs))
    tops = _merge_top(rows[0], pad(rows[1]))
    rest = [x for r in rows[2:] for x in r]
    for base in range(0, len(rest), k):
        tops = _merge_top(tops, _sort_desc(pad(rest[base:base + k])))
    return tops


def _peer_route_body(q_ref, k1_ref, k2_ref, d1_ref, s2_ref, e1_ref, e2_ref):
    n_keys = k1_ref.shape[1]
    dq = k1_ref.shape[2]
    for h in range(k1_ref.shape[0]):
        q1 = q_ref[:, (2 * h) * dq:(2 * h + 1) * dq]
        q2 = q_ref[:, (2 * h + 1) * dq:(2 * h + 2) * dq]
        s1 = lax.dot_general(k1_ref[h], q1, _NT, preferred_element_type=F32)
        s2 = lax.dot_general(k2_ref[h], q2, _NT, preferred_element_type=F32)
        v1 = _top_keys(s1)
        v2 = _top_keys(s2)
        top = _grid_top(v1, v2)
        z = None
        for c in top:
            e = jnp.exp(c - top[0])
            z = e if z is None else z + e
        bc = lambda x: jnp.broadcast_to(x[0:1, :], (n_keys, x.shape[1]))
        d1_ref[h] = bc(top[PEER_TOPK - 1]) - s1
        s2_ref[h] = s2
        e1_ref[h] = jnp.exp(s1 - bc(v1[0]))
        e2_ref[h] = jnp.exp(s2 - bc(v2[0])) / bc(z)


def peer_route(q, k1, k2, *, tt):
    s, dqt = q.shape
    h, n_keys, dq = k1.shape
    out = jax.ShapeDtypeStruct((h, n_keys, s), F32)
    spec = pl.BlockSpec((h, n_keys, tt), lambda i: (0, 0, i))
    kspec = pl.BlockSpec((h, n_keys, dq), lambda i: (0, 0, 0))
    return pl.pallas_call(
        _peer_route_body,
        grid=(s // tt,),
        in_specs=[pl.BlockSpec((tt, dqt), lambda i: (i, 0)), kspec, kspec],
        out_specs=[spec] * 4,
        out_shape=[out] * 4,
        compiler_params=_params("parallel"),
        name="peer_route",
    )(q, k1, k2)


def _peer_mix_body(n_ref, u_ref, vt_ref, d1_ref, s2_ref, e1_ref, e2_ref, x_ref, o_ref,
                   acc_ref, at0_ref, at1_ref, h0_ref, h1_ref, *, n_blocks):
    i = pl.program_id(0)
    j = pl.program_id(1)
    te, tt = h0_ref.shape
    n_keys = s2_ref.shape[1]
    d = acc_ref.shape[0]
    n_chunks = te // n_keys
    half = tt // 2

    @pl.when((i == 0) & (j == 0))
    def _():
        at0_ref[...] = jnp.zeros(at0_ref.shape, F32)
        at1_ref[...] = jnp.zeros(at1_ref.shape, F32)
        h0_ref[...] = jnp.zeros(h0_ref.shape, h0_ref.dtype)
        h1_ref[...] = jnp.zeros(h1_ref.shape, h1_ref.dtype)

    @pl.when(j == 0)
    def _():
        acc_ref[...] = jnp.zeros(acc_ref.shape, F32)

    def stages(at_w, at_r, h_w, h_r, row0):
        weights_valid = (j >= 1) & (j <= n_blocks)
        kq = d // 4

        def act_piece(t0, r0, k0):
            def run():
                part = lax.dot_general(u_ref[r0:r0 + MXU_DIM, k0:k0 + kq], n_ref[t0:t0 + half, k0:k0 + kq],
                                       _NT, preferred_element_type=F32)
                if k0 == 0:
                    at_w[r0:r0 + MXU_DIM, t0:t0 + half] = part
                else:
                    at_w[r0:r0 + MXU_DIM, t0:t0 + half] += part
            return run

        def val_piece(o0, t0):
            def run():
                acc_ref[o0:o0 + MXU_DIM, t0:t0 + half] += jnp.dot(
                    vt_ref[o0:o0 + MXU_DIM, :], h_r[:, t0:t0 + half], preferred_element_type=F32)
            return run

        def weight_slab(r, c, k0):
            def run():
                cols = slice(c, c + LANES)
                keys = slice(k0, k0 + SLAB_KEYS)
                w = jnp.zeros((SLAB_KEYS, LANES), F32)
                for h in range(s2_ref.shape[0]):
                    d1 = d1_ref[h, row0 + r:row0 + r + 1, cols]
                    e1v = e1_ref[h, row0 + r:row0 + r + 1, cols]
                    w = w + jnp.where(s2_ref[h, keys, cols] >= d1, e2_ref[h, keys, cols] * e1v, 0.0)
                out = slice(r * n_keys + k0, r * n_keys + k0 + SLAB_KEYS)
                hv = jax.nn.gelu(at_r[out, cols]) * w
                h_w[out, cols] = jnp.where(weights_valid, hv, 0.0).astype(h_w.dtype)
            return run

        acts = [act_piece(t0, r0, k0) for t0 in range(0, tt, half) for r0 in range(0, te, MXU_DIM)
                for k0 in range(0, d, kq)]
        vals = [val_piece(o0, t0) for o0 in range(0, d, MXU_DIM) for t0 in range(0, tt, half)]
        slabs = [weight_slab(r, c, k0) for r in range(n_chunks) for c in range(0, tt, LANES)
                 for k0 in range(0, n_keys, SLAB_KEYS)]
        mxu = [p for pair in zip(acts, vals) for p in pair]
        per = len(slabs) // len(mxu)
        for idx, piece in enumerate(mxu):
            piece()
            for slab in slabs[idx * per:(idx + 1) * per]:
                slab()

    @pl.when(j % 2 == 0)
    def _():
        stages(at0_ref, at1_ref, h1_ref, h0_ref, n_chunks)

    @pl.when(j % 2 == 1)
    def _():
        stages(at1_ref, at0_ref, h0_ref, h1_ref, 0)

    @pl.when(j == pl.num_programs(1) - 1)
    def _():
        o_ref[...] = x_ref[...] + acc_ref[...].T


def peer_mix(n, u_tab, vt_tab, route, x, *, tt, te):
    s, d = n.shape
    e = u_tab.shape[0]
    nb = e // te
    d1, s2, e1, e2 = route
    h, n_keys, _ = s2.shape
    assert 2 * (te // n_keys) == SUBLANES
    clamp = lambda b: jnp.clip(b, 0, nb - 1)
    rspec = pl.BlockSpec((h, n_keys, tt), lambda i, j: (0, 0, i))
    gspec = pl.BlockSpec((h, SUBLANES, tt), lambda i, j: (0, clamp(j - 1) // 2, i))
    return pl.pallas_call(
        functools.partial(_peer_mix_body, n_blocks=nb),
        grid=(s // tt, nb + 2),
        in_specs=[
            pl.BlockSpec((tt, d), lambda i, j: (i, 0)),
            pl.BlockSpec((te, d), lambda i, j: (clamp(j), 0)),
            pl.BlockSpec((d, te), lambda i, j: (0, clamp(j - 2))),
            gspec, rspec, gspec, rspec,
            pl.BlockSpec((tt, d), lambda i, j: (i, 0)),
        ],
        out_specs=pl.BlockSpec((tt, d), lambda i, j: (i, 0)),
        out_shape=jax.ShapeDtypeStruct((s, d), F32),
        scratch_shapes=[
            pltpu.VMEM((d, tt), F32),
            pltpu.VMEM((te, tt), F32), pltpu.VMEM((te, tt), F32),
            pltpu.VMEM((te, tt), BF16), pltpu.VMEM((te, tt), BF16),
        ],
        compiler_params=_params("arbitrary", "arbitrary"),
        name="peer_mix",
    )(n, u_tab, vt_tab, *route, x)


def kernel(x, mem, norm_mix_g, w_in, ssm_a_re, ssm_a_im, ssm_log_step, ssm_b_re, ssm_b_im, ssm_c_re, ssm_c_im, ssm_d, ssm_w_glu, ssm_b_glu, ssm_out_g, att_lq1, att_lk1, att_lq2, att_lk2, att_subln_g, w_out, norm_mem_g, mem_norm_g, mem_w_q, mem_w_k, mem_w_v, mem_w_o, norm_ffn_g, peer_w_q, peer_k1, peer_k2, peer_u, peer_v, final_g):
    bsz, s, d = x.shape
    assert bsz == 1
    depth = w_in.shape[0]
    d_ssm = ssm_d.shape[1]
    d_att = (w_in.shape[2] - d_ssm) // 3
    n_att_heads = d_att // ATT_V_DIM
    tm = min(512, s)

    xs = x.reshape(s, d).astype(F32)
    mem2 = mem.reshape(mem.shape[1], d).astype(F32)
    for l in range(depth):
        proj = norm_matmul(xs, norm_mix_g[l], w_in[l].astype(BF16), tm=tm, tn=1024)
        y_ssm = s5_group(proj[:, :d_ssm], ssm_a_re[l], ssm_a_im[l], ssm_log_step[l], ssm_b_re[l], ssm_b_im[l],
                         ssm_c_re[l], ssm_c_im[l], ssm_d[l], ssm_w_glu[l], ssm_b_glu[l], ssm_out_g[l])
        lam_init = 0.8 - 0.6 * math.exp(-0.3 * l)
        lam = (jnp.exp(jnp.sum(att_lq1[l].astype(F32) * att_lk1[l].astype(F32)))
               - jnp.exp(jnp.sum(att_lq2[l].astype(F32) * att_lk2[l].astype(F32))) + lam_init)
        y_att = diff_attention(proj, lam, att_subln_g[l], n_heads=n_att_heads, q_col=d_ssm, k_col=d_ssm + d_att,
                               v_col=d_ssm + 2 * d_att, lam_init=lam_init, tq=tm)
        w_o = w_out[l].astype(BF16)
        xs = matmul_residual([y_ssm, y_att], [w_o[:d_ssm], w_o[d_ssm:]], xs, tm=tm, tn=1024)

        q = norm_matmul(xs, norm_mem_g[l], mem_w_q[l].astype(BF16), tm=tm, tn=1024)
        w_kv = jnp.concatenate([mem_w_k[l], mem_w_v[l]], axis=1).astype(BF16)
        kv = norm_matmul(mem2, mem_norm_g[l], w_kv, tm=mem2.shape[0], tn=1024)
        o = mem_attention(q, kv, n_heads=N_MEM_HEADS, tm=tm)
        xs = matmul_residual([o], [mem_w_o[l].astype(BF16)], xs, tm=tm, tn=1024)

        pq, n = norm_matmul(xs, norm_ffn_g[l], peer_w_q[l].astype(BF16), tm=tm, tn=1024, return_norm=True)
        route = peer_route(pq, peer_k1[l].astype(BF16), peer_k2[l].astype(BF16), tt=min(256, s))
        xs = peer_mix(n, peer_u[l].astype(BF16), peer_v[l].astype(BF16).T, route, xs, tt=tm, te=512)
    return final_norm(xs, final_g, tm=tm).reshape(bsz, s, d)
```

```python
import functools
import math

import jax
import jax.numpy as jnp
from jax import lax
from jax.experimental import pallas as pl
from jax.experimental.pallas import tpu as pltpu

F32 = jnp.float32
BF16 = jnp.bfloat16

EPS = 1e-6
LANES = 128
SUBLANES = 8
MXU_DIM = 256
VMEM_LIMIT_BYTES = 56 * 1024 * 1024

CHUNK = 64
SSM_GROUP = 16
SSM_STATE = 64
SSM_BLOCK = 32
ATT_HEAD_DIM = 64
ATT_V_DIM = 2 * ATT_HEAD_DIM
N_MEM_HEADS = 4
PEER_HEADS = 8
PEER_N_KEYS = 128
PEER_D_HALF = 128
PEER_TOPK = 16
SLAB_KEYS = 32

_NT = (((1,), (1,)), ((), ()))


def _params(*sem):
    return pltpu.CompilerParams(dimension_semantics=sem, vmem_limit_bytes=VMEM_LIMIT_BYTES)


def _rms(x, g):
    return x * lax.rsqrt(jnp.mean(x * x, axis=-1, keepdims=True) + EPS) * g


def _norm_matmul_body(x_ref, g_ref, w_ref, o_ref, xn_ref):
    @pl.when(pl.program_id(1) == 0)
    def _():
        xn_ref[...] = _rms(x_ref[...], g_ref[...]).astype(xn_ref.dtype)

    o_ref[...] = jnp.dot(xn_ref[...], w_ref[...], preferred_element_type=F32).astype(o_ref.dtype)


def norm_matmul(x, g, w, *, tm, tn, return_norm=False):
    m, k = x.shape
    n = w.shape[1]
    out_specs = [pl.BlockSpec((tm, tn), lambda i, j: (i, j))]
    out_shape = [jax.ShapeDtypeStruct((m, n), BF16)]
    scratch = [pltpu.VMEM((tm, k), BF16)]
    if return_norm:
        out_specs.append(pl.BlockSpec((tm, k), lambda i, j: (i, 0)))
        out_shape.append(jax.ShapeDtypeStruct((m, k), BF16))
        scratch = []
    res = pl.pallas_call(
        _norm_matmul_body,
        grid=(m // tm, n // tn),
        in_specs=[
            pl.BlockSpec((tm, k), lambda i, j: (i, 0)),
            pl.BlockSpec((1, k), lambda i, j: (0, 0)),
            pl.BlockSpec((k, tn), lambda i, j: (0, j)),
        ],
        out_specs=out_specs,
        out_shape=out_shape,
        scratch_shapes=scratch,
        compiler_params=_params("parallel", "arbitrary"),
        name="norm_matmul",
    )(x, g.reshape(1, k).astype(F32), w)
    return res if return_norm else res[0]


def _matmul_residual_body(*refs, n_terms):
    x_ref, o_ref = refs[2 * n_terms], refs[2 * n_terms + 1]
    acc = x_ref[...]
    for y_ref, w_ref in zip(refs[:n_terms], refs[n_terms:2 * n_terms]):
        acc = acc + jnp.dot(y_ref[...], w_ref[...], preferred_element_type=F32)
    o_ref[...] = acc


def matmul_residual(ys, ws, x, *, tm, tn):
    m, n = x.shape
    in_specs = [pl.BlockSpec((tm, y.shape[1]), lambda i, j: (i, 0)) for y in ys]
    in_specs += [pl.BlockSpec((w.shape[0], tn), lambda i, j: (0, j)) for w in ws]
    in_specs += [pl.BlockSpec((tm, tn), lambda i, j: (i, j))]
    return pl.pallas_call(
        functools.partial(_matmul_residual_body, n_terms=len(ys)),
        grid=(m // tm, n // tn),
        in_specs=in_specs,
        out_specs=pl.BlockSpec((tm, tn), lambda i, j: (i, j)),
        out_shape=jax.ShapeDtypeStruct((m, n), F32),
        compiler_params=_params("parallel", "parallel"),
        name="matmul_residual",
    )(*ys, *ws, x)


def _final_norm_body(x_ref, g_ref, o_ref):
    o_ref[...] = _rms(x_ref[...], g_ref[...])


def final_norm(x, g, *, tm):
    m, k = x.shape
    return pl.pallas_call(
        _final_norm_body,
        grid=(m // tm,),
        in_specs=[pl.BlockSpec((tm, k), lambda i: (i, 0)), pl.BlockSpec((1, k), lambda i: (0, 0))],
        out_specs=pl.BlockSpec((tm, k), lambda i: (i, 0)),
        out_shape=jax.ShapeDtypeStruct((m, k), F32),
        compiler_params=_params("parallel"),
        name="final_norm",
    )(x, g.reshape(1, k).astype(F32))


def _s5_kernel_matrix_body(x_ref, y_ref, o_ref, *, block):
    m = jnp.dot(x_ref[0], y_ref[0], preferred_element_type=F32, precision=lax.Precision.HIGHEST)
    rows = lax.broadcasted_iota(jnp.int32, m.shape, 0) // SSM_GROUP
    cols = lax.broadcasted_iota(jnp.int32, m.shape, 1) // SSM_GROUP
    o_ref[0] = jnp.where(rows <= cols, m, 0.0).astype(o_ref.dtype)


def s5_kernel_matrix(yt, xt):
    g, lc, p2 = yt.shape
    return pl.pallas_call(
        functools.partial(_s5_kernel_matrix_body, block=SSM_BLOCK),
        grid=(g,),
        in_specs=[pl.BlockSpec((1, lc, p2), lambda i: (i, 0, 0)), pl.BlockSpec((1, p2, lc), lambda i: (i, 0, 0))],
        out_specs=pl.BlockSpec((1, lc, lc), lambda i: (i, 0, 0)),
        out_shape=jax.ShapeDtypeStruct((g, lc, lc), BF16),
        compiler_params=_params("parallel"),
        name="s5_kernel_matrix",
    )(yt, xt)


def _s5_block_input_body(u_ref, bx_ref, z_ref):
    z_ref[0] = jnp.dot(u_ref[0], bx_ref[0], preferred_element_type=F32)


def s5_block_input(ug, bx):
    g, nb, lc = ug.shape
    p2 = bx.shape[2]
    return pl.pallas_call(
        _s5_block_input_body,
        grid=(g,),
        in_specs=[pl.BlockSpec((1, nb, lc), lambda i: (i, 0, 0)), pl.BlockSpec((1, lc, p2), lambda i: (i, 0, 0))],
        out_specs=pl.BlockSpec((1, nb, p2), lambda i: (i, 0, 0)),
        out_shape=jax.ShapeDtypeStruct((g, nb, p2), F32),
        compiler_params=_params("parallel"),
        name="s5_block_input",
    )(ug, bx)


def _s5_scan_body(z_ref, a1_ref, a2_ref, h_ref, state_ref, *, half):
    @pl.when(pl.program_id(0) == 0)
    def _():
        state_ref[...] = jnp.zeros(state_ref.shape, F32)

    a1 = a1_ref[...]
    a2 = a2_ref[...]

    def step(k, state):
        h_ref[k] = state
        return state * a1 + pltpu.roll(state, half, 1) * a2 + z_ref[k]

    state_ref[...] = lax.fori_loop(0, z_ref.shape[0], step, state_ref[...])


def s5_scan(zt, a1, a2, *, tb):
    nb, g, p2 = zt.shape
    return pl.pallas_call(
        functools.partial(_s5_scan_body, half=p2 // 2),
        grid=(nb // tb,),
        in_specs=[
            pl.BlockSpec((tb, g, p2), lambda i: (i, 0, 0)),
            pl.BlockSpec((g, p2), lambda i: (0, 0)),
            pl.BlockSpec((g, p2), lambda i: (0, 0)),
        ],
        out_specs=pl.BlockSpec((tb, g, p2), lambda i: (i, 0, 0)),
        out_shape=jax.ShapeDtypeStruct((nb, g, p2), F32),
        scratch_shapes=[pltpu.VMEM((g, p2), F32)],
        compiler_params=_params("arbitrary"),
        name="s5_scan",
    )(zt, a1, a2)


def _s5_output_body(u_ref, mt_ref, h_ref, cx_ref, d_ref, y_ref):
    u = u_ref[0]
    y = jnp.dot(u, mt_ref[0], preferred_element_type=F32)
    y = y + jnp.dot(h_ref[0], cx_ref[0], preferred_element_type=F32)
    y = y + d_ref[0] * u.astype(F32)
    y_ref[0] = jax.nn.gelu(y).astype(y_ref.dtype)


def s5_output(ug, mt, hg, cx, dg):
    g, nb, lc = ug.shape
    p2 = hg.shape[2]
    return pl.pallas_call(
        _s5_output_body,
        grid=(g,),
        in_specs=[
            pl.BlockSpec((1, nb, lc), lambda i: (i, 0, 0)),
            pl.BlockSpec((1, lc, lc), lambda i: (i, 0, 0)),
            pl.BlockSpec((1, nb, p2), lambda i: (i, 0, 0)),
            pl.BlockSpec((1, p2, lc), lambda i: (i, 0, 0)),
            pl.BlockSpec((1, 1, lc), lambda i: (i, 0, 0)),
        ],
        out_specs=pl.BlockSpec((1, nb, lc), lambda i: (i, 0, 0)),
        out_shape=jax.ShapeDtypeStruct((g, nb, lc), BF16),
        compiler_params=_params("parallel"),
        name="s5_output",
    )(ug, mt, hg, cx, dg)


def _glu_norm_body(y_ref, w_ref, b_ref, g_ref, o_ref):
    y = y_ref[...]
    z = jnp.dot(y, w_ref[...], preferred_element_type=F32) + b_ref[...]
    o = y.astype(F32) * jax.nn.sigmoid(z)
    o_ref[...] = _rms(o, g_ref[...]).astype(o_ref.dtype)


def glu_norm(y, w, b, g, *, tm):
    m, k = y.shape
    return pl.pallas_call(
        _glu_norm_body,
        grid=(m // tm,),
        in_specs=[
            pl.BlockSpec((tm, k), lambda i: (i, 0)),
            pl.BlockSpec((k, k), lambda i: (0, 0)),
            pl.BlockSpec((1, k), lambda i: (0, 0)),
            pl.BlockSpec((1, k), lambda i: (0, 0)),
        ],
        out_specs=pl.BlockSpec((tm, k), lambda i: (i, 0)),
        out_shape=jax.ShapeDtypeStruct((m, k), BF16),
        compiler_params=_params("parallel"),
        name="glu_norm",
    )(y, w, b.reshape(1, k).astype(F32), g.reshape(1, k).astype(F32))


def s5_group(u, a_re, a_im, log_step, b_re, b_im, c_re, c_im, d_skip, w_glu, b_glu, out_g):
    s, d_ssm = u.shape
    g, p = a_re.shape
    c = SSM_GROUP
    blk = SSM_BLOCK
    nb = s // blk
    lc = blk * c

    a = lax.complex(a_re.astype(F32), a_im.astype(F32))
    step = jnp.exp(log_step.astype(F32))[:, None]
    log_abar = a * step
    a_bar = jnp.exp(log_abar)
    b_bar = ((a_bar - 1.0) / a)[..., None] * lax.complex(b_re.astype(F32), b_im.astype(F32))
    cm = lax.complex(c_re.astype(F32), c_im.astype(F32))
    pos = jnp.arange(blk, dtype=F32)

    def apow(e):
        return jnp.exp(log_abar[:, None, :] * e[None, :, None])

    xm = (apow(pos)[:, :, None, :] * cm[:, None, :, :]).reshape(g, lc, p)
    ym = (jnp.swapaxes(apow(-pos), 1, 2)[:, :, :, None] * b_bar[:, :, None, :]).reshape(g, p, lc)
    x_ri = jnp.concatenate([jnp.real(xm), -jnp.imag(xm)], axis=2)
    y_ri = jnp.concatenate([jnp.real(ym), jnp.imag(ym)], axis=1)
    mt = s5_kernel_matrix(jnp.swapaxes(y_ri, 1, 2), jnp.swapaxes(x_ri, 1, 2))

    bxm = (apow(blk - 1.0 - pos)[:, :, None, :] * jnp.swapaxes(b_bar, 1, 2)[:, None, :, :]).reshape(g, lc, p)
    bx = jnp.concatenate([jnp.real(bxm), jnp.imag(bxm)], axis=2).astype(BF16)
    cxm = jnp.swapaxes((apow(pos + 1.0)[:, :, None, :] * cm[:, None, :, :]).reshape(g, lc, p), 1, 2)
    cx = jnp.concatenate([jnp.real(cxm), -jnp.imag(cxm)], axis=1).astype(BF16)
    a_blk = jnp.exp(log_abar * blk)
    a1 = jnp.concatenate([jnp.real(a_blk), jnp.real(a_blk)], axis=1)
    a2 = jnp.concatenate([-jnp.imag(a_blk), jnp.imag(a_blk)], axis=1)
    dg = jnp.tile(d_skip.astype(F32).reshape(g, 1, c), (1, blk, 1)).reshape(g, 1, lc)

    ug = u.reshape(nb, blk, g, c).transpose(2, 0, 1, 3).reshape(g, nb, lc)
    z = s5_block_input(ug, bx)
    h = s5_scan(z.transpose(1, 0, 2), a1, a2, tb=min(64, nb))
    hg = h.transpose(1, 0, 2).astype(BF16)
    yg = s5_output(ug, mt, hg, cx, dg)
    y = yg.reshape(g, nb, blk, c).transpose(1, 2, 0, 3).reshape(s, d_ssm)
    return glu_norm(y, w_glu.astype(BF16), b_glu, out_g, tm=min(512, s))


def _diff_attn_body(lam_ref, q_ref, k_ref, v_ref, g_ref, o_ref, m_scr, l_scr, acc_scr, *, tq, out_scale):
    i = pl.program_id(1)
    q = (q_ref[...].astype(F32) * (ATT_HEAD_DIM ** -0.5 * math.log2(math.e))).astype(q_ref.dtype)
    lane = lax.broadcasted_iota(jnp.int32, q.shape, 1)
    zero = jnp.zeros_like(q)
    qq = jnp.concatenate([jnp.where(lane < ATT_HEAD_DIM, q, zero), jnp.where(lane >= ATT_HEAD_DIM, q, zero)], axis=0)

    m_scr[...] = jnp.full(m_scr.shape, -jnp.inf, F32)
    l_scr[...] = jnp.zeros(l_scr.shape, F32)
    acc_scr[...] = jnp.zeros(acc_scr.shape, F32)
    reps = tq // LANES

    def step(start, masked):
        kb = k_ref[pl.ds(start, tq), :]
        vb = v_ref[pl.ds(start, tq), :]
        s = lax.dot_general(qq, kb, _NT, preferred_element_type=F32)
        if masked:
            qc = (lax.broadcasted_iota(jnp.int32, s.shape, 0) % tq) // CHUNK
            kc = lax.broadcasted_iota(jnp.int32, s.shape, 1) // CHUNK
            s = jnp.where(kc <= qc, s, -jnp.inf)
        m_prev = m_scr[...]
        m_next = jnp.maximum(m_prev, jnp.max(s, axis=1, keepdims=True))
        alpha = jnp.exp2(m_prev - m_next)
        p = jnp.exp2(s - jnp.concatenate([m_next] * reps, axis=1))
        l_scr[...] = alpha * l_scr[...] + jnp.sum(p, axis=1, keepdims=True)
        acc_scr[...] = alpha * acc_scr[...] + jnp.dot(p.astype(vb.dtype), vb, preferred_element_type=F32)
        m_scr[...] = m_next

    def full_pair(j, carry):
        step(pl.multiple_of(2 * j * tq, tq), False)
        step(pl.multiple_of((2 * j + 1) * tq, tq), False)
        return carry

    lax.fori_loop(0, lax.shift_right_logical(i, 1), full_pair, 0)
    diag = pl.multiple_of(i * tq, tq)

    @pl.when((i & 1) == 1)
    def _():
        step(pl.multiple_of((i - 1) * tq, tq), False)
        step(diag, True)

    @pl.when((i & 1) == 0)
    def _():
        step(diag, True)

    o = acc_scr[...] / l_scr[...]
    d = o[:tq] - lam_ref[0] * o[tq:]
    o_ref[...] = (_rms(d, g_ref[...]) * out_scale).astype(o_ref.dtype)


def diff_attention(proj, lam, subln_g, *, n_heads, q_col, k_col, v_col, lam_init, tq):
    s = proj.shape[0]
    dv = ATT_V_DIM
    return pl.pallas_call(
        functools.partial(_diff_attn_body, tq=tq, out_scale=1.0 - lam_init),
        grid=(n_heads, s // tq),
        in_specs=[
            pl.BlockSpec(memory_space=pltpu.SMEM),
            pl.BlockSpec((tq, dv), lambda h, i: (i, q_col // dv + h)),
            pl.BlockSpec((s, dv), lambda h, i: (0, k_col // dv + h)),
            pl.BlockSpec((s, dv), lambda h, i: (0, v_col // dv + h)),
            pl.BlockSpec((1, dv), lambda h, i: (0, 0)),
        ],
        out_specs=pl.BlockSpec((tq, dv), lambda h, i: (i, h)),
        out_shape=jax.ShapeDtypeStruct((s, n_heads * dv), BF16),
        scratch_shapes=[
            pltpu.VMEM((2 * tq, LANES), F32),
            pltpu.VMEM((2 * tq, LANES), F32),
            pltpu.VMEM((2 * tq, dv), F32),
        ],
        compiler_params=_params("parallel", "arbitrary"),
        name="diff_attention",
    )(lam.reshape(1).astype(F32), proj, proj, proj, subln_g.reshape(1, dv).astype(F32))


def _mem_attn_body(q_ref, k_ref, v_ref, o_ref, *, scale):
    s = lax.dot_general(q_ref[...], k_ref[...], _NT, preferred_element_type=F32) * scale
    p = jnp.exp(s - jnp.max(s, axis=1, keepdims=True))
    p = p / jnp.sum(p, axis=1, keepdims=True)
    o_ref[...] = jnp.dot(p.astype(v_ref.dtype), v_ref[...], preferred_element_type=F32).astype(o_ref.dtype)


def mem_attention(q, kv, *, n_heads, tm):
    s, d = q.shape
    m = kv.shape[0]
    dh = d // n_heads
    return pl.pallas_call(
        functools.partial(_mem_attn_body, scale=dh ** -0.5),
        grid=(s // tm, n_heads),
        in_specs=[
            pl.BlockSpec((tm, dh), lambda i, h: (i, h)),
            pl.BlockSpec((m, dh), lambda i, h: (0, h)),
            pl.BlockSpec((m, dh), lambda i, h: (0, n_heads + h)),
        ],
        out_specs=pl.BlockSpec((tm, dh), lambda i, h: (i, h)),
        out_shape=jax.ShapeDtypeStruct((s, d), BF16),
        compiler_params=_params("parallel", "parallel"),
        name="mem_attention",
    )(q, kv, kv)


def _cmpx(xs, i, j):
    a, b = xs[i], xs[j]
    if b is None:
        return
    if a is None:
        xs[i], xs[j] = b, None
        return
    xs[i], xs[j] = jnp.maximum(a, b), jnp.minimum(a, b)


def _bitonic_merge_desc(xs):
    n = len(xs)
    j = n // 2
    while j >= 1:
        for i in range(n):
            if (i ^ j) > i:
                _cmpx(xs, i, i ^ j)
        j //= 2
    return xs


def _sort_desc(xs):
    n = len(xs)
    k = 2
    while k <= n:
        j = k // 2
        while j >= 1:
            for i in range(n):
                l = i ^ j
                if l > i:
                    if (i & k) == 0:
                        _cmpx(xs, i, l)
                    else:
                        _cmpx(xs, l, i)
            j //= 2
        k *= 2
    return xs


def _max_or_none(a, b):
    if a is None:
        return b
    if b is None:
        return a
    return jnp.maximum(a, b)


def _merge_top(a, b):
    n = len(a)
    return _bitonic_merge_desc([_max_or_none(a[i], b[n - 1 - i]) for i in range(n)])


def _top_keys(st):
    n_keys = st.shape[0]
    k = PEER_TOPK
    parts = [st[r * SUBLANES:(r + 1) * SUBLANES, :] for r in range(n_keys // SUBLANES)]
    tops = None
    for base in range(0, len(parts), k):
        grp = _sort_desc(parts[base:base + k])
        tops = grp if tops is None else _merge_top(tops, grp)
    shift = SUBLANES // 2
    while shift >= 1:
        tops = _merge_top(tops, [pltpu.roll(x, shift, 0) for x in tops])
        shift //= 2
    return tops


def _grid_top(v1, v2):
    k = PEER_TOPK
    rows = [[v1[i] + v2[j] for j in range(k // (i + 1))] for i in range(k)]
    pad = lambda xs: xs + [None] * (k - len(xs))
    tops = _merge_top(rows[0], pad(rows[1]))
    rest = [x for r in rows[2:] for x in r]
    for base in range(0, len(rest), k):
        tops = _merge_top(tops, _sort_desc(pad(rest[base:base + k])))
    return tops


def _peer_route_body(q_ref, k1_ref, k2_ref, d1_ref, s2_ref, e1_ref, e2_ref):
    n_keys = k1_ref.shape[1]
    dq = k1_ref.shape[2]
    for h in range(k1_ref.shape[0]):
        q1 = q_ref[:, (2 * h) * dq:(2 * h + 1) * dq]
        q2 = q_ref[:, (2 * h + 1) * dq:(2 * h + 2) * dq]
        s1 = lax.dot_general(k1_ref[h], q1, _NT, preferred_element_type=F32)
        s2 = lax.dot_general(k2_ref[h], q2, _NT, preferred_element_type=F32)
        v1 = _top_keys(s1)
        v2 = _top_keys(s2)
        top = _grid_top(v1, v2)
        z = None
        for c in top:
            e = jnp.exp(c - top[0])
            z = e if z is None else z + e
        bc = lambda x: jnp.broadcast_to(x[0:1, :], (n_keys, x.shape[1]))
        d1_ref[h] = bc(top[PEER_TOPK - 1]) - s1
        s2_ref[h] = s2
        e1_ref[h] = jnp.exp(s1 - bc(v1[0]))
        e2_ref[h] = jnp.exp(s2 - bc(v2[0])) / bc(z)


def peer_route(q, k1, k2, *, tt):
    s, dqt = q.shape
    h, n_keys, dq = k1.shape
    out = jax.ShapeDtypeStruct((h, n_keys, s), F32)
    spec = pl.BlockSpec((h, n_keys, tt), lambda i: (0, 0, i))
    kspec = pl.BlockSpec((h, n_keys, dq), lambda i: (0, 0, 0))
    return pl.pallas_call(
        _peer_route_body,
        grid=(s // tt,),
        in_specs=[pl.BlockSpec((tt, dqt), lambda i: (i, 0)), kspec, kspec],
        out_specs=[spec] * 4,
        out_shape=[out] * 4,
        compiler_params=_params("parallel"),
        name="peer_route",
    )(q, k1, k2)


def _peer_mix_body(n_ref, u_ref, vt_ref, d1_ref, s2_ref, e1_ref, e2_ref, x_ref, o_ref,
                   acc_ref, at0_ref, at1_ref, h0_ref, h1_ref, *, n_blocks):
    i = pl.program_id(0)
    j = pl.program_id(1)
    te, tt = h0_ref.shape
    n_keys = s2_ref.shape[1]
    d = acc_ref.shape[0]
    n_chunks = te // n_keys
    half = tt // 2

    @pl.when((i == 0) & (j == 0))
    def _():
        at0_ref[...] = jnp.zeros(at0_ref.shape, F32)
        at1_ref[...] = jnp.zeros(at1_ref.shape, F32)
        h0_ref[...] = jnp.zeros(h0_ref.shape, h0_ref.dtype)
        h1_ref[...] = jnp.zeros(h1_ref.shape, h1_ref.dtype)

    @pl.when(j == 0)
    def _():
        acc_ref[...] = jnp.zeros(acc_ref.shape, F32)

    def stages(at_w, at_r, h_w, h_r, row0):
        weights_valid = (j >= 1) & (j <= n_blocks)
        kq = d // 4

        def act_piece(t0, r0, k0):
            def run():
                part = lax.dot_general(u_ref[r0:r0 + MXU_DIM, k0:k0 + kq], n_ref[t0:t0 + half, k0:k0 + kq],
                                       _NT, preferred_element_type=F32)
                if k0 == 0:
                    at_w[r0:r0 + MXU_DIM, t0:t0 + half] = part
                else:
                    at_w[r0:r0 + MXU_DIM, t0:t0 + half] += part
            return run

        def val_piece(o0, t0):
            def run():
                acc_ref[o0:o0 + MXU_DIM, t0:t0 + half] += jnp.dot(
                    vt_ref[o0:o0 + MXU_DIM, :], h_r[:, t0:t0 + half], preferred_element_type=F32)
            return run

        def weight_slab(r, c, k0):
            def run():
                cols = slice(c, c + LANES)
                keys = slice(k0, k0 + SLAB_KEYS)
                w = jnp.zeros((SLAB_KEYS, LANES), F32)
                for h in range(s2_ref.shape[0]):
                    d1 = d1_ref[h, row0 + r:row0 + r + 1, cols]
                    e1v = e1_ref[h, row0 + r:row0 + r + 1, cols]
                    w = w + jnp.where(s2_ref[h, keys, cols] >= d1, e2_ref[h, keys, cols] * e1v, 0.0)
                out = slice(r * n_keys + k0, r * n_keys + k0 + SLAB_KEYS)
                hv = jax.nn.gelu(at_r[out, cols]) * w
                h_w[out, cols] = jnp.where(weights_valid, hv, 0.0).astype(h_w.dtype)
            return run

        acts = [act_piece(t0, r0, k0) for t0 in range(0, tt, half) for r0 in range(0, te, MXU_DIM)
                for k0 in range(0, d, kq)]
        vals = [val_piece(o0, t0) for o0 in range(0, d, MXU_DIM) for t0 in range(0, tt, half)]
        slabs = [weight_slab(r, c, k0) for r in range(n_chunks) for c in range(0, tt, LANES)
                 for k0 in range(0, n_keys, SLAB_KEYS)]
        mxu = [p for pair in zip(acts, vals) for p in pair]
        per = len(slabs) // len(mxu)
        for idx, piece in enumerate(mxu):
            piece()
            for slab in slabs[idx * per:(idx + 1) * per]:
                slab()

    @pl.when(j % 2 == 0)
    def _():
        stages(at0_ref, at1_ref, h1_ref, h0_ref, n_chunks)

    @pl.when(j % 2 == 1)
    def _():
        stages(at1_ref, at0_ref, h0_ref, h1_ref, 0)

    @pl.when(j == pl.num_programs(1) - 1)
    def _():
        o_ref[...] = x_ref[...] + acc_ref[...].T


def peer_mix(n, u_tab, vt_tab, route, x, *, tt, te):
    s, d = n.shape
    e = u_tab.shape[0]
    nb = e // te
    assert vt_tab.shape == (nb, d, te)
    d1, s2, e1, e2 = route
    h, n_keys, _ = s2.shape
    assert 2 * (te // n_keys) == SUBLANES
    clamp = lambda b: jnp.clip(b, 0, nb - 1)
    rspec = pl.BlockSpec((h, n_keys, tt), lambda i, j: (0, 0, i))
    gspec = pl.BlockSpec((h, SUBLANES, tt), lambda i, j: (0, clamp(j - 1) // 2, i))
    return pl.pallas_call(
        functools.partial(_peer_mix_body, n_blocks=nb),
        grid=(s // tt, nb + 2),
        in_specs=[
            pl.BlockSpec((tt, d), lambda i, j: (i, 0)),
            pl.BlockSpec((te, d), lambda i, j: (clamp(j), 0)),
            pl.BlockSpec((None, d, te), lambda i, j: (clamp(j - 2), 0, 0)),
            gspec, rspec, gspec, rspec,
            pl.BlockSpec((tt, d), lambda i, j: (i, 0)),
        ],
        out_specs=pl.BlockSpec((tt, d), lambda i, j: (i, 0)),
        out_shape=jax.ShapeDtypeStruct((s, d), F32),
        scratch_shapes=[
            pltpu.VMEM((d, tt), F32),
            pltpu.VMEM((te, tt), F32), pltpu.VMEM((te, tt), F32),
            pltpu.VMEM((te, tt), BF16), pltpu.VMEM((te, tt), BF16),
        ],
        compiler_params=_params("arbitrary", "arbitrary"),
        name="peer_mix",
    )(n, u_tab, vt_tab, *route, x)


def kernel(x, mem, norm_mix_g, w_in, ssm_a_re, ssm_a_im, ssm_log_step, ssm_b_re, ssm_b_im, ssm_c_re, ssm_c_im, ssm_d, ssm_w_glu, ssm_b_glu, ssm_out_g, att_lq1, att_lk1, att_lq2, att_lk2, att_subln_g, w_out, norm_mem_g, mem_norm_g, mem_w_q, mem_w_k, mem_w_v, mem_w_o, norm_ffn_g, peer_w_q, peer_k1, peer_k2, peer_u, peer_v, final_g):
    bsz, s, d = x.shape
    assert bsz == 1
    depth = w_in.shape[0]
    d_ssm = ssm_d.shape[1]
    d_att = (w_in.shape[2] - d_ssm) // 3
    n_att_heads = d_att // ATT_V_DIM
    tm = min(512, s)

    xs = x.reshape(s, d).astype(F32)
    mem2 = mem.reshape(mem.shape[1], d).astype(F32)
    for l in range(depth):
        proj = norm_matmul(xs, norm_mix_g[l], w_in[l].astype(BF16), tm=tm, tn=1024)
        y_ssm = s5_group(proj[:, :d_ssm], ssm_a_re[l], ssm_a_im[l], ssm_log_step[l], ssm_b_re[l], ssm_b_im[l],
                         ssm_c_re[l], ssm_c_im[l], ssm_d[l], ssm_w_glu[l], ssm_b_glu[l], ssm_out_g[l])
        lam_init = 0.8 - 0.6 * math.exp(-0.3 * l)
        lam = (jnp.exp(jnp.sum(att_lq1[l].astype(F32) * att_lk1[l].astype(F32)))
               - jnp.exp(jnp.sum(att_lq2[l].astype(F32) * att_lk2[l].astype(F32))) + lam_init)
        y_att = diff_attention(proj, lam, att_subln_g[l], n_heads=n_att_heads, q_col=d_ssm, k_col=d_ssm + d_att,
                               v_col=d_ssm + 2 * d_att, lam_init=lam_init, tq=tm)
        w_o = w_out[l].astype(BF16)
        xs = matmul_residual([y_ssm, y_att], [w_o[:d_ssm], w_o[d_ssm:]], xs, tm=tm, tn=1024)

        q = norm_matmul(xs, norm_mem_g[l], mem_w_q[l].astype(BF16), tm=tm, tn=1024)
        w_kv = jnp.concatenate([mem_w_k[l], mem_w_v[l]], axis=1).astype(BF16)
        kv = norm_matmul(mem2, mem_norm_g[l], w_kv, tm=mem2.shape[0], tn=1024)
        o = mem_attention(q, kv, n_heads=N_MEM_HEADS, tm=tm)
        xs = matmul_residual([o], [mem_w_o[l].astype(BF16)], xs, tm=tm, tn=1024)

        pq, n = norm_matmul(xs, norm_ffn_g[l], peer_w_q[l].astype(BF16), tm=tm, tn=1024, return_norm=True)
        route = peer_route(pq, peer_k1[l].astype(BF16), peer_k2[l].astype(BF16), tt=min(256, s))
        te = 512
        vt = peer_v[l].astype(BF16).reshape(-1, te, d).transpose(0, 2, 1)
        xs = peer_mix(n, peer_u[l].astype(BF16), vt, route, xs, tt=tm, te=te)
    return final_norm(xs, final_g, tm=tm).reshape(bsz, s, d)
```

```python
import functools
import math

import jax
import jax.numpy as jnp
from jax import lax
from jax.experimental import pallas as pl
from jax.experimental.pallas import tpu as pltpu

F32 = jnp.float32
BF16 = jnp.bfloat16

EPS = 1e-6
LANES = 128
SUBLANES = 8
MXU_DIM = 256
VMEM_LIMIT_BYTES = 56 * 1024 * 1024

CHUNK = 64
SSM_GROUP = 16
SSM_STATE = 64
SSM_BLOCK = 32
ATT_HEAD_DIM = 64
ATT_V_DIM = 2 * ATT_HEAD_DIM
N_MEM_HEADS = 4
PEER_HEADS = 8
PEER_N_KEYS = 128
PEER_D_HALF = 128
PEER_TOPK = 16
SLAB_KEYS = 32

_NT = (((1,), (1,)), ((), ()))


def _params(*sem):
    return pltpu.CompilerParams(dimension_semantics=sem, vmem_limit_bytes=VMEM_LIMIT_BYTES)


def _rms(x, g):
    return x * lax.rsqrt(jnp.mean(x * x, axis=-1, keepdims=True) + EPS) * g


def _norm_matmul_body(x_ref, g_ref, w_ref, o_ref, xn_ref):
    @pl.when(pl.program_id(1) == 0)
    def _():
        xn_ref[...] = _rms(x_ref[...], g_ref[...]).astype(xn_ref.dtype)

    o_ref[...] = jnp.dot(xn_ref[...], w_ref[...], preferred_element_type=F32).astype(o_ref.dtype)


def norm_matmul(x, g, w, *, tm, tn, return_norm=False):
    m, k = x.shape
    n = w.shape[1]
    out_specs = [pl.BlockSpec((tm, tn), lambda i, j: (i, j))]
    out_shape = [jax.ShapeDtypeStruct((m, n), BF16)]
    scratch = [pltpu.VMEM((tm, k), BF16)]
    if return_norm:
        out_specs.append(pl.BlockSpec((tm, k), lambda i, j: (i, 0)))
        out_shape.append(jax.ShapeDtypeStruct((m, k), BF16))
        scratch = []
    res = pl.pallas_call(
        _norm_matmul_body,
        grid=(m // tm, n // tn),
        in_specs=[
            pl.BlockSpec((tm, k), lambda i, j: (i, 0)),
            pl.BlockSpec((1, k), lambda i, j: (0, 0)),
            pl.BlockSpec((k, tn), lambda i, j: (0, j)),
        ],
        out_specs=out_specs,
        out_shape=out_shape,
        scratch_shapes=scratch,
        compiler_params=_params("parallel", "arbitrary"),
        name="norm_matmul",
    )(x, g.reshape(1, k).astype(F32), w)
    return res if return_norm else res[0]


def _matmul_residual_body(*refs, n_terms):
    x_ref, o_ref = refs[2 * n_terms], refs[2 * n_terms + 1]
    acc = x_ref[...]
    for y_ref, w_ref in zip(refs[:n_terms], refs[n_terms:2 * n_terms]):
        acc = acc + jnp.dot(y_ref[...], w_ref[...], preferred_element_type=F32)
    o_ref[...] = acc


def matmul_residual(ys, ws, x, *, tm, tn):
    m, n = x.shape
    in_specs = [pl.BlockSpec((tm, y.shape[1]), lambda i, j: (i, 0)) for y in ys]
    in_specs += [pl.BlockSpec((w.shape[0], tn), lambda i, j: (0, j)) for w in ws]
    in_specs += [pl.BlockSpec((tm, tn), lambda i, j: (i, j))]
    return pl.pallas_call(
        functools.partial(_matmul_residual_body, n_terms=len(ys)),
        grid=(m // tm, n // tn),
        in_specs=in_specs,
        out_specs=pl.BlockSpec((tm, tn), lambda i, j: (i, j)),
        out_shape=jax.ShapeDtypeStruct((m, n), F32),
        compiler_params=_params("parallel", "parallel"),
        name="matmul_residual",
    )(*ys, *ws, x)


def _final_norm_body(x_ref, g_ref, o_ref):
    o_ref[...] = _rms(x_ref[...], g_ref[...])


def final_norm(x, g, *, tm):
    m, k = x.shape
    return pl.pallas_call(
        _final_norm_body,
        grid=(m // tm,),
        in_specs=[pl.BlockSpec((tm, k), lambda i: (i, 0)), pl.BlockSpec((1, k), lambda i: (0, 0))],
        out_specs=pl.BlockSpec((tm, k), lambda i: (i, 0)),
        out_shape=jax.ShapeDtypeStruct((m, k), F32),
        compiler_params=_params("parallel"),
        name="final_norm",
    )(x, g.reshape(1, k).astype(F32))


def _s5_kernel_matrix_body(x_ref, y_ref, o_ref, *, block):
    m = jnp.dot(x_ref[0], y_ref[0], preferred_element_type=F32, precision=lax.Precision.HIGHEST)
    rows = lax.broadcasted_iota(jnp.int32, m.shape, 0) // SSM_GROUP
    cols = lax.broadcasted_iota(jnp.int32, m.shape, 1) // SSM_GROUP
    o_ref[0] = jnp.where(rows <= cols, m, 0.0).astype(o_ref.dtype)


def s5_kernel_matrix(yt, xt):
    g, lc, p2 = yt.shape
    return pl.pallas_call(
        functools.partial(_s5_kernel_matrix_body, block=SSM_BLOCK),
        grid=(g,),
        in_specs=[pl.BlockSpec((1, lc, p2), lambda i: (i, 0, 0)), pl.BlockSpec((1, p2, lc), lambda i: (i, 0, 0))],
        out_specs=pl.BlockSpec((1, lc, lc), lambda i: (i, 0, 0)),
        out_shape=jax.ShapeDtypeStruct((g, lc, lc), BF16),
        compiler_params=_params("parallel"),
        name="s5_kernel_matrix",
    )(yt, xt)


def _s5_block_input_body(u_ref, bx_ref, z_ref):
    z_ref[0] = jnp.dot(u_ref[0], bx_ref[0], preferred_element_type=F32)


def s5_block_input(ug, bx):
    g, nb, lc = ug.shape
    p2 = bx.shape[2]
    return pl.pallas_call(
        _s5_block_input_body,
        grid=(g,),
        in_specs=[pl.BlockSpec((1, nb, lc), lambda i: (i, 0, 0)), pl.BlockSpec((1, lc, p2), lambda i: (i, 0, 0))],
        out_specs=pl.BlockSpec((1, nb, p2), lambda i: (i, 0, 0)),
        out_shape=jax.ShapeDtypeStruct((g, nb, p2), F32),
        compiler_params=_params("parallel"),
        name="s5_block_input",
    )(ug, bx)


def _s5_scan_body(z_ref, a1_ref, a2_ref, h_ref, state_ref, *, half):
    @pl.when(pl.program_id(0) == 0)
    def _():
        state_ref[...] = jnp.zeros(state_ref.shape, F32)

    a1 = a1_ref[...]
    a2 = a2_ref[...]

    def step(k, state):
        h_ref[k] = state
        return state * a1 + pltpu.roll(state, half, 1) * a2 + z_ref[k]

    state_ref[...] = lax.fori_loop(0, z_ref.shape[0], step, state_ref[...])


def s5_scan(zt, a1, a2, *, tb):
    nb, g, p2 = zt.shape
    return pl.pallas_call(
        functools.partial(_s5_scan_body, half=p2 // 2),
        grid=(nb // tb,),
        in_specs=[
            pl.BlockSpec((tb, g, p2), lambda i: (i, 0, 0)),
            pl.BlockSpec((g, p2), lambda i: (0, 0)),
            pl.BlockSpec((g, p2), lambda i: (0, 0)),
        ],
        out_specs=pl.BlockSpec((tb, g, p2), lambda i: (i, 0, 0)),
        out_shape=jax.ShapeDtypeStruct((nb, g, p2), F32),
        scratch_shapes=[pltpu.VMEM((g, p2), F32)],
        compiler_params=_params("arbitrary"),
        name="s5_scan",
    )(zt, a1, a2)


def _s5_output_body(u_ref, mt_ref, h_ref, cx_ref, d_ref, y_ref):
    u = u_ref[0]
    y = jnp.dot(u, mt_ref[0], preferred_element_type=F32)
    y = y + jnp.dot(h_ref[0], cx_ref[0], preferred_element_type=F32)
    y = y + d_ref[0] * u.astype(F32)
    y_ref[0] = jax.nn.gelu(y).astype(y_ref.dtype)


def s5_output(ug, mt, hg, cx, dg):
    g, nb, lc = ug.shape
    p2 = hg.shape[2]
    return pl.pallas_call(
        _s5_output_body,
        grid=(g,),
        in_specs=[
            pl.BlockSpec((1, nb, lc), lambda i: (i, 0, 0)),
            pl.BlockSpec((1, lc, lc), lambda i: (i, 0, 0)),
            pl.BlockSpec((1, nb, p2), lambda i: (i, 0, 0)),
            pl.BlockSpec((1, p2, lc), lambda i: (i, 0, 0)),
            pl.BlockSpec((1, 1, lc), lambda i: (i, 0, 0)),
        ],
        out_specs=pl.BlockSpec((1, nb, lc), lambda i: (i, 0, 0)),
        out_shape=jax.ShapeDtypeStruct((g, nb, lc), BF16),
        compiler_params=_params("parallel"),
        name="s5_output",
    )(ug, mt, hg, cx, dg)


def _glu_norm_body(y_ref, w_ref, b_ref, g_ref, o_ref):
    y = y_ref[...]
    z = jnp.dot(y, w_ref[...], preferred_element_type=F32) + b_ref[...]
    o = y.astype(F32) * jax.nn.sigmoid(z)
    o_ref[...] = _rms(o, g_ref[...]).astype(o_ref.dtype)


def glu_norm(y, w, b, g, *, tm):
    m, k = y.shape
    return pl.pallas_call(
        _glu_norm_body,
        grid=(m // tm,),
        in_specs=[
            pl.BlockSpec((tm, k), lambda i: (i, 0)),
            pl.BlockSpec((k, k), lambda i: (0, 0)),
            pl.BlockSpec((1, k), lambda i: (0, 0)),
            pl.BlockSpec((1, k), lambda i: (0, 0)),
        ],
        out_specs=pl.BlockSpec((tm, k), lambda i: (i, 0)),
        out_shape=jax.ShapeDtypeStruct((m, k), BF16),
        compiler_params=_params("parallel"),
        name="glu_norm",
    )(y, w, b.reshape(1, k).astype(F32), g.reshape(1, k).astype(F32))


def s5_group(u, a_re, a_im, log_step, b_re, b_im, c_re, c_im, d_skip, w_glu, b_glu, out_g):
    s, d_ssm = u.shape
    g, p = a_re.shape
    c = SSM_GROUP
    blk = SSM_BLOCK
    nb = s // blk
    lc = blk * c

    a = lax.complex(a_re.astype(F32), a_im.astype(F32))
    step = jnp.exp(log_step.astype(F32))[:, None]
    log_abar = a * step
    a_bar = jnp.exp(log_abar)
    b_bar = ((a_bar - 1.0) / a)[..., None] * lax.complex(b_re.astype(F32), b_im.astype(F32))
    cm = lax.complex(c_re.astype(F32), c_im.astype(F32))
    pos = jnp.arange(blk, dtype=F32)

    def apow(e):
        return jnp.exp(log_abar[:, None, :] * e[None, :, None])

    xm = (apow(pos)[:, :, None, :] * cm[:, None, :, :]).reshape(g, lc, p)
    ym = (jnp.swapaxes(apow(-pos), 1, 2)[:, :, :, None] * b_bar[:, :, None, :]).reshape(g, p, lc)
    x_ri = jnp.concatenate([jnp.real(xm), -jnp.imag(xm)], axis=2)
    y_ri = jnp.concatenate([jnp.real(ym), jnp.imag(ym)], axis=1)
    mt = s5_kernel_matrix(jnp.swapaxes(y_ri, 1, 2), jnp.swapaxes(x_ri, 1, 2))

    bxm = (apow(blk - 1.0 - pos)[:, :, None, :] * jnp.swapaxes(b_bar, 1, 2)[:, None, :, :]).reshape(g, lc, p)
    bx = jnp.concatenate([jnp.real(bxm), jnp.imag(bxm)], axis=2).astype(BF16)
    cxm = jnp.swapaxes((apow(pos + 1.0)[:, :, None, :] * cm[:, None, :, :]).reshape(g, lc, p), 1, 2)
    cx = jnp.concatenate([jnp.real(cxm), -jnp.imag(cxm)], axis=1).astype(BF16)
    a_blk = jnp.exp(log_abar * blk)
    a1 = jnp.concatenate([jnp.real(a_blk), jnp.real(a_blk)], axis=1)
    a2 = jnp.concatenate([-jnp.imag(a_blk), jnp.imag(a_blk)], axis=1)
    dg = jnp.tile(d_skip.astype(F32).reshape(g, 1, c), (1, blk, 1)).reshape(g, 1, lc)

    ug = u.reshape(nb, blk, g, c).transpose(2, 0, 1, 3).reshape(g, nb, lc)
    z = s5_block_input(ug, bx)
    h = s5_scan(z.transpose(1, 0, 2), a1, a2, tb=min(64, nb))
    hg = h.transpose(1, 0, 2).astype(BF16)
    yg = s5_output(ug, mt, hg, cx, dg)
    y = yg.reshape(g, nb, blk, c).transpose(1, 2, 0, 3).reshape(s, d_ssm)
    return glu_norm(y, w_glu.astype(BF16), b_glu, out_g, tm=min(512, s))


def _diff_attn_body(lam_ref, q_ref, k_ref, v_ref, g_ref, o_ref, m_scr, l_scr, acc_scr, *, tq, out_scale):
    i = pl.program_id(1)
    q = (q_ref[...].astype(F32) * (ATT_HEAD_DIM ** -0.5 * math.log2(math.e))).astype(q_ref.dtype)
    lane = lax.broadcasted_iota(jnp.int32, q.shape, 1)
    zero = jnp.zeros_like(q)
    qq = jnp.concatenate([jnp.where(lane < ATT_HEAD_DIM, q, zero), jnp.where(lane >= ATT_HEAD_DIM, q, zero)], axis=0)

    m_scr[...] = jnp.full(m_scr.shape, -jnp.inf, F32)
    l_scr[...] = jnp.zeros(l_scr.shape, F32)
    acc_scr[...] = jnp.zeros(acc_scr.shape, F32)
    reps = tq // LANES

    def step(start, masked):
        kb = k_ref[pl.ds(start, tq), :]
        vb = v_ref[pl.ds(start, tq), :]
        s = lax.dot_general(qq, kb, _NT, preferred_element_type=F32)
        if masked:
            qc = (lax.broadcasted_iota(jnp.int32, s.shape, 0) % tq) // CHUNK
            kc = lax.broadcasted_iota(jnp.int32, s.shape, 1) // CHUNK
            s = jnp.where(kc <= qc, s, -jnp.inf)
        m_prev = m_scr[...]
        m_next = jnp.maximum(m_prev, jnp.max(s, axis=1, keepdims=True))
        alpha = jnp.exp2(m_prev - m_next)
        p = jnp.exp2(s - jnp.concatenate([m_next] * reps, axis=1))
        l_scr[...] = alpha * l_scr[...] + jnp.sum(p, axis=1, keepdims=True)
        acc_scr[...] = alpha * acc_scr[...] + jnp.dot(p.astype(vb.dtype), vb, preferred_element_type=F32)
        m_scr[...] = m_next

    def full_pair(j, carry):
        step(pl.multiple_of(2 * j * tq, tq), False)
        step(pl.multiple_of((2 * j + 1) * tq, tq), False)
        return carry

    lax.fori_loop(0, lax.shift_right_logical(i, 1), full_pair, 0)
    diag = pl.multiple_of(i * tq, tq)

    @pl.when((i & 1) == 1)
    def _():
        step(pl.multiple_of((i - 1) * tq, tq), False)
        step(diag, True)

    @pl.when((i & 1) == 0)
    def _():
        step(diag, True)

    o = acc_scr[...] / l_scr[...]
    d = o[:tq] - lam_ref[0] * o[tq:]
    o_ref[...] = (_rms(d, g_ref[...]) * out_scale).astype(o_ref.dtype)


def diff_attention(proj, lam, subln_g, *, n_heads, q_col, k_col, v_col, lam_init, tq):
    s = proj.shape[0]
    dv = ATT_V_DIM
    return pl.pallas_call(
        functools.partial(_diff_attn_body, tq=tq, out_scale=1.0 - lam_init),
        grid=(n_heads, s // tq),
        in_specs=[
            pl.BlockSpec(memory_space=pltpu.SMEM),
            pl.BlockSpec((tq, dv), lambda h, i: (i, q_col // dv + h)),
            pl.BlockSpec((s, dv), lambda h, i: (0, k_col // dv + h)),
            pl.BlockSpec((s, dv), lambda h, i: (0, v_col // dv + h)),
            pl.BlockSpec((1, dv), lambda h, i: (0, 0)),
        ],
        out_specs=pl.BlockSpec((tq, dv), lambda h, i: (i, h)),
        out_shape=jax.ShapeDtypeStruct((s, n_heads * dv), BF16),
        scratch_shapes=[
            pltpu.VMEM((2 * tq, LANES), F32),
            pltpu.VMEM((2 * tq, LANES), F32),
            pltpu.VMEM((2 * tq, dv), F32),
        ],
        compiler_params=_params("parallel", "arbitrary"),
        name="diff_attention",
    )(lam.reshape(1).astype(F32), proj, proj, proj, subln_g.reshape(1, dv).astype(F32))


def _mem_attn_body(q_ref, k_ref, v_ref, o_ref, *, scale):
    s = lax.dot_general(q_ref[...], k_ref[...], _NT, preferred_element_type=F32) * scale
    p = jnp.exp(s - jnp.max(s, axis=1, keepdims=True))
    p = p / jnp.sum(p, axis=1, keepdims=True)
    o_ref[...] = jnp.dot(p.astype(v_ref.dtype), v_ref[...], preferred_element_type=F32).astype(o_ref.dtype)


def mem_attention(q, kv, *, n_heads, tm):
    s, d = q.shape
    m = kv.shape[0]
    dh = d // n_heads
    return pl.pallas_call(
        functools.partial(_mem_attn_body, scale=dh ** -0.5),
        grid=(s // tm, n_heads),
        in_specs=[
            pl.BlockSpec((tm, dh), lambda i, h: (i, h)),
            pl.BlockSpec((m, dh), lambda i, h: (0, h)),
            pl.BlockSpec((m, dh), lambda i, h: (0, n_heads + h)),
        ],
        out_specs=pl.BlockSpec((tm, dh), lambda i, h: (i, h)),
        out_shape=jax.ShapeDtypeStruct((s, d), BF16),
        compiler_params=_params("parallel", "parallel"),
        name="mem_attention",
    )(q, kv, kv)


def _cmpx(xs, i, j):
    a, b = xs[i], xs[j]
    if b is None:
        return
    if a is None:
        xs[i], xs[j] = b, None
        return
    xs[i], xs[j] = jnp.maximum(a, b), jnp.minimum(a, b)


def _bitonic_merge_desc(xs):
    n = len(xs)
    j = n // 2
    while j >= 1:
        for i in range(n):
            if (i ^ j) > i:
                _cmpx(xs, i, i ^ j)
        j //= 2
    return xs


def _sort_desc(xs):
    n = len(xs)
    k = 2
    while k <= n:
        j = k // 2
        while j >= 1:
            for i in range(n):
                l = i ^ j
                if l > i:
                    if (i & k) == 0:
                        _cmpx(xs, i, l)
                    else:
                        _cmpx(xs, l, i)
            j //= 2
        k *= 2
    return xs


def _max_or_none(a, b):
    if a is None:
        return b
    if b is None:
        return a
    return jnp.maximum(a, b)


def _merge_top(a, b):
    n = len(a)
    return _bitonic_merge_desc([_max_or_none(a[i], b[n - 1 - i]) for i in range(n)])


def _top_keys(st):
    n_keys = st.shape[0]
    k = PEER_TOPK
    parts = [st[r * SUBLANES:(r + 1) * SUBLANES, :] for r in range(n_keys // SUBLANES)]
    tops = None
    for base in range(0, len(parts), k):
        grp = _sort_desc(parts[base:base + k])
        tops = grp if tops is None else _merge_top(tops, grp)
    shift = SUBLANES // 2
    while shift >= 1:
        tops = _merge_top(tops, [pltpu.roll(x, shift, 0) for x in tops])
        shift //= 2
    return tops


def _grid_top(v1, v2):
    k = PEER_TOPK
    rows = [[v1[i] + v2[j] for j in range(k // (i + 1))] for i in range(k)]
    pad = lambda xs: xs + [None] * (k - len(xs))
    tops = _merge_top(rows[0], pad(rows[1]))
    rest = [x for r in rows[2:] for x in r]
    for base in range(0, len(rest), k):
        tops = _merge_top(tops, _sort_desc(pad(rest[base:base + k])))
    return tops


def _peer_route_body(q_ref, k1_ref, k2_ref, d1_ref, s2_ref, e1_ref, e2_ref):
    n_keys = k1_ref.shape[1]
    dq = k1_ref.shape[2]
    for h in range(k1_ref.shape[0]):
        q1 = q_ref[:, (2 * h) * dq:(2 * h + 1) * dq]
        q2 = q_ref[:, (2 * h + 1) * dq:(2 * h + 2) * dq]
        s1 = lax.dot_general(k1_ref[h], q1, _NT, preferred_element_type=F32)
        s2 = lax.dot_general(k2_ref[h], q2, _NT, preferred_element_type=F32)
        v1 = _top_keys(s1)
        v2 = _top_keys(s2)
        top = _grid_top(v1, v2)
        z = None
        for c in top:
            e = jnp.exp(c - top[0])
            z = e if z is None else z + e
        bc = lambda x: jnp.broadcast_to(x[0:1, :], (n_keys, x.shape[1]))
        d1_ref[h] = bc(top[PEER_TOPK - 1]) - s1
        s2_ref[h] = s2
        e1_ref[h] = jnp.exp(s1 - bc(v1[0]))
        e2_ref[h] = jnp.exp(s2 - bc(v2[0])) / bc(z)


def peer_route(q, k1, k2, *, tt):
    s, dqt = q.shape
    h, n_keys, dq = k1.shape
    out = jax.ShapeDtypeStruct((h, n_keys, s), F32)
    spec = pl.BlockSpec((h, n_keys, tt), lambda i: (0, 0, i))
    kspec = pl.BlockSpec((h, n_keys, dq), lambda i: (0, 0, 0))
    return pl.pallas_call(
        _peer_route_body,
        grid=(s // tt,),
        in_specs=[pl.BlockSpec((tt, dqt), lambda i: (i, 0)), kspec, kspec],
        out_specs=[spec] * 4,
        out_shape=[out] * 4,
        compiler_params=_params("parallel"),
        name="peer_route",
    )(q, k1, k2)


def _peer_mix_body(n_ref, u_ref, vt_ref, d1_ref, s2_ref, e1_ref, e2_ref, x_ref, o_ref,
                   acc_ref, at0_ref, at1_ref, h0_ref, h1_ref, *, n_blocks):
    i = pl.program_id(0)
    j = pl.program_id(1)
    te, tt = h0_ref.shape
    n_keys = s2_ref.shape[1]
    d = acc_ref.shape[0]
    n_chunks = te // n_keys
    half = tt // 2

    @pl.when((i == 0) & (j == 0))
    def _():
        at0_ref[...] = jnp.zeros(at0_ref.shape, F32)
        at1_ref[...] = jnp.zeros(at1_ref.shape, F32)
        h0_ref[...] = jnp.zeros(h0_ref.shape, h0_ref.dtype)
        h1_ref[...] = jnp.zeros(h1_ref.shape, h1_ref.dtype)

    @pl.when(j == 0)
    def _():
        acc_ref[...] = jnp.zeros(acc_ref.shape, F32)

    def stages(at_w, at_r, h_w, h_r, row0):
        weights_valid = (j >= 1) & (j <= n_blocks)
        kq = d // 4

        def act_piece(t0, r0, k0):
            def run():
                part = lax.dot_general(u_ref[r0:r0 + MXU_DIM, k0:k0 + kq], n_ref[t0:t0 + half, k0:k0 + kq],
                                       _NT, preferred_element_type=F32)
                if k0 == 0:
                    at_w[r0:r0 + MXU_DIM, t0:t0 + half] = part
                else:
                    at_w[r0:r0 + MXU_DIM, t0:t0 + half] += part
            return run

        def val_piece(o0, t0):
            def run():
                acc_ref[o0:o0 + MXU_DIM, t0:t0 + half] += jnp.dot(
                    vt_ref[o0:o0 + MXU_DIM, :], h_r[:, t0:t0 + half], preferred_element_type=F32)
            return run

        def weight_slab(r, c, k0):
            def run():
                cols = slice(c, c + LANES)
                keys = slice(k0, k0 + SLAB_KEYS)
                w = jnp.zeros((SLAB_KEYS, LANES), F32)
                for h in range(s2_ref.shape[0]):
                    d1 = d1_ref[h, row0 + r:row0 + r + 1, cols]
                    e1v = e1_ref[h, row0 + r:row0 + r + 1, cols]
                    w = w + jnp.where(s2_ref[h, keys, cols] >= d1, e2_ref[h, keys, cols] * e1v, 0.0)
                out = slice(r * n_keys + k0, r * n_keys + k0 + SLAB_KEYS)
                hv = jax.nn.gelu(at_r[out, cols]) * w
                h_w[out, cols] = jnp.where(weights_valid, hv, 0.0).astype(h_w.dtype)
            return run

        acts = [act_piece(t0, r0, k0) for t0 in range(0, tt, half) for r0 in range(0, te, MXU_DIM)
                for k0 in range(0, d, kq)]
        vals = [val_piece(o0, t0) for o0 in range(0, d, MXU_DIM) for t0 in range(0, tt, half)]
        slabs = [weight_slab(r, c, k0) for r in range(n_chunks) for c in range(0, tt, LANES)
                 for k0 in range(0, n_keys, SLAB_KEYS)]
        mxu = [p for pair in zip(acts, vals) for p in pair]
        per = len(slabs) // len(mxu)
        for idx, piece in enumerate(mxu):
            piece()
            for slab in slabs[idx * per:(idx + 1) * per]:
                slab()

    @pl.when(j % 2 == 0)
    def _():
        stages(at0_ref, at1_ref, h1_ref, h0_ref, n_chunks)

    @pl.when(j % 2 == 1)
    def _():
        stages(at1_ref, at0_ref, h0_ref, h1_ref, 0)

    @pl.when(j == pl.num_programs(1) - 1)
    def _():
        o_ref[...] = x_ref[...] + acc_ref[...].T


def peer_mix(n, u_tab, vt_tab, route, x, *, tt, te):
    s, d = n.shape
    e = u_tab.shape[0]
    nb = e // te
    assert vt_tab.shape == (nb, d, te)
    d1, s2, e1, e2 = route
    h, n_keys, _ = s2.shape
    assert 2 * (te // n_keys) == SUBLANES
    clamp = lambda b: jnp.clip(b, 0, nb - 1)
    rspec = pl.BlockSpec((h, n_keys, tt), lambda i, j: (0, 0, i))
    gspec = pl.BlockSpec((h, SUBLANES, tt), lambda i, j: (0, clamp(j - 1) // 2, i))
    return pl.pallas_call(
        functools.partial(_peer_mix_body, n_blocks=nb),
        grid=(s // tt, nb + 2),
        in_specs=[
            pl.BlockSpec((tt, d), lambda i, j: (i, 0)),
            pl.BlockSpec((te, d), lambda i, j: (clamp(j), 0)),
            pl.BlockSpec((None, d, te), lambda i, j: (clamp(j - 2), 0, 0)),
            gspec, rspec, gspec, rspec,
            pl.BlockSpec((tt, d), lambda i, j: (i, 0)),
        ],
        out_specs=pl.BlockSpec((tt, d), lambda i, j: (i, 0)),
        out_shape=jax.ShapeDtypeStruct((s, d), F32),
        scratch_shapes=[
            pltpu.VMEM((d, tt), F32),
            pltpu.VMEM((te, tt), F32), pltpu.VMEM((te, tt), F32),
            pltpu.VMEM((te, tt), BF16), pltpu.VMEM((te, tt), BF16),
        ],
        compiler_params=_params("arbitrary", "arbitrary"),
        name="peer_mix",
    )(n, u_tab, vt_tab, *route, x)


def kernel(x, mem, norm_mix_g, w_in, ssm_a_re, ssm_a_im, ssm_log_step, ssm_b_re, ssm_b_im, ssm_c_re, ssm_c_im, ssm_d, ssm_w_glu, ssm_b_glu, ssm_out_g, att_lq1, att_lk1, att_lq2, att_lk2, att_subln_g, w_out, norm_mem_g, mem_norm_g, mem_w_q, mem_w_k, mem_w_v, mem_w_o, norm_ffn_g, peer_w_q, peer_k1, peer_k2, peer_u, peer_v, final_g):
    bsz, s, d = x.shape
    assert bsz == 1
    depth = w_in.shape[0]
    d_ssm = ssm_d.shape[1]
    d_att = (w_in.shape[2] - d_ssm) // 3
    n_att_heads = d_att // ATT_V_DIM
    tm = min(512, s)
    tp = min(1024, s)
    tn = 2048

    xs = x.reshape(s, d).astype(F32)
    mem2 = mem.reshape(mem.shape[1], d).astype(F32)
    for l in range(depth):
        proj = norm_matmul(xs, norm_mix_g[l], w_in[l].astype(BF16), tm=tp, tn=tn)
        y_ssm = s5_group(proj[:, :d_ssm], ssm_a_re[l], ssm_a_im[l], ssm_log_step[l], ssm_b_re[l], ssm_b_im[l],
                         ssm_c_re[l], ssm_c_im[l], ssm_d[l], ssm_w_glu[l], ssm_b_glu[l], ssm_out_g[l])
        lam_init = 0.8 - 0.6 * math.exp(-0.3 * l)
        lam = (jnp.exp(jnp.sum(att_lq1[l].astype(F32) * att_lk1[l].astype(F32)))
               - jnp.exp(jnp.sum(att_lq2[l].astype(F32) * att_lk2[l].astype(F32))) + lam_init)
        y_att = diff_attention(proj, lam, att_subln_g[l], n_heads=n_att_heads, q_col=d_ssm, k_col=d_ssm + d_att,
                               v_col=d_ssm + 2 * d_att, lam_init=lam_init, tq=tp)
        w_o = w_out[l].astype(BF16)
        xs = matmul_residual([y_ssm, y_att], [w_o[:d_ssm], w_o[d_ssm:]], xs, tm=tm, tn=tn)

        q = norm_matmul(xs, norm_mem_g[l], mem_w_q[l].astype(BF16), tm=tp, tn=tn)
        w_kv = jnp.concatenate([mem_w_k[l], mem_w_v[l]], axis=1).astype(BF16)
        kv = norm_matmul(mem2, mem_norm_g[l], w_kv, tm=mem2.shape[0], tn=tn)
        o = mem_attention(q, kv, n_heads=N_MEM_HEADS, tm=tm)
        xs = matmul_residual([o], [mem_w_o[l].astype(BF16)], xs, tm=tm, tn=tn)

        pq, n = norm_matmul(xs, norm_ffn_g[l], peer_w_q[l].astype(BF16), tm=tp, tn=tn, return_norm=True)
        route = peer_route(pq, peer_k1[l].astype(BF16), peer_k2[l].astype(BF16), tt=min(256, s))
        te = 512
        vt = peer_v[l].astype(BF16).reshape(-1, te, d).transpose(0, 2, 1)
        xs = peer_mix(n, peer_u[l].astype(BF16), vt, route, xs, tt=tm, te=te)
    return final_norm(xs, final_g, tm=tm).reshape(bsz, s, d)
```

```python
import functools
import math

import jax
import jax.numpy as jnp
from jax import lax
from jax.experimental import pallas as pl
from jax.experimental.pallas import tpu as pltpu

F32 = jnp.float32
BF16 = jnp.bfloat16

EPS = 1e-6
LANES = 128
SUBLANES = 8
MXU_DIM = 256
VMEM_LIMIT_BYTES = 56 * 1024 * 1024

CHUNK = 64
SSM_GROUP = 16
SSM_STATE = 64
SSM_BLOCK = 32
ATT_HEAD_DIM = 64
ATT_V_DIM = 2 * ATT_HEAD_DIM
N_MEM_HEADS = 4
PEER_HEADS = 8
PEER_N_KEYS = 128
PEER_D_HALF = 128
PEER_TOPK = 16
SLAB_KEYS = 32

_NT = (((1,), (1,)), ((), ()))


def _params(*sem):
    return pltpu.CompilerParams(dimension_semantics=sem, vmem_limit_bytes=VMEM_LIMIT_BYTES)


def _rms(x, g):
    return x * lax.rsqrt(jnp.mean(x * x, axis=-1, keepdims=True) + EPS) * g


def _norm_matmul_body(x_ref, g_ref, w_ref, o_ref, xn_ref):
    @pl.when(pl.program_id(1) == 0)
    def _():
        xn_ref[...] = _rms(x_ref[...], g_ref[...]).astype(xn_ref.dtype)

    o_ref[...] = jnp.dot(xn_ref[...], w_ref[...], preferred_element_type=F32).astype(o_ref.dtype)


def norm_matmul(x, g, w, *, tm, tn, return_norm=False):
    m, k = x.shape
    n = w.shape[1]
    out_specs = [pl.BlockSpec((tm, tn), lambda i, j: (i, j))]
    out_shape = [jax.ShapeDtypeStruct((m, n), BF16)]
    scratch = [pltpu.VMEM((tm, k), BF16)]
    if return_norm:
        out_specs.append(pl.BlockSpec((tm, k), lambda i, j: (i, 0)))
        out_shape.append(jax.ShapeDtypeStruct((m, k), BF16))
        scratch = []
    res = pl.pallas_call(
        _norm_matmul_body,
        grid=(m // tm, n // tn),
        in_specs=[
            pl.BlockSpec((tm, k), lambda i, j: (i, 0)),
            pl.BlockSpec((1, k), lambda i, j: (0, 0)),
            pl.BlockSpec((k, tn), lambda i, j: (0, j)),
        ],
        out_specs=out_specs,
        out_shape=out_shape,
        scratch_shapes=scratch,
        compiler_params=_params("parallel", "arbitrary"),
        name="norm_matmul",
    )(x, g.reshape(1, k).astype(F32), w)
    return res if return_norm else res[0]


def _matmul_residual_body(*refs, n_terms):
    x_ref, o_ref = refs[2 * n_terms], refs[2 * n_terms + 1]
    acc = x_ref[...]
    for y_ref, w_ref in zip(refs[:n_terms], refs[n_terms:2 * n_terms]):
        acc = acc + jnp.dot(y_ref[...], w_ref[...], preferred_element_type=F32)
    o_ref[...] = acc


def matmul_residual(ys, ws, x, *, tm, tn):
    m, n = x.shape
    in_specs = [pl.BlockSpec((tm, y.shape[1]), lambda i, j: (i, 0)) for y in ys]
    in_specs += [pl.BlockSpec((w.shape[0], tn), lambda i, j: (0, j)) for w in ws]
    in_specs += [pl.BlockSpec((tm, tn), lambda i, j: (i, j))]
    return pl.pallas_call(
        functools.partial(_matmul_residual_body, n_terms=len(ys)),
        grid=(m // tm, n // tn),
        in_specs=in_specs,
        out_specs=pl.BlockSpec((tm, tn), lambda i, j: (i, j)),
        out_shape=jax.ShapeDtypeStruct((m, n), F32),
        compiler_params=_params("parallel", "parallel"),
        name="matmul_residual",
    )(*ys, *ws, x)


def _final_norm_body(x_ref, g_ref, o_ref):
    o_ref[...] = _rms(x_ref[...], g_ref[...])


def final_norm(x, g, *, tm):
    m, k = x.shape
    return pl.pallas_call(
        _final_norm_body,
        grid=(m // tm,),
        in_specs=[pl.BlockSpec((tm, k), lambda i: (i, 0)), pl.BlockSpec((1, k), lambda i: (0, 0))],
        out_specs=pl.BlockSpec((tm, k), lambda i: (i, 0)),
        out_shape=jax.ShapeDtypeStruct((m, k), F32),
        compiler_params=_params("parallel"),
        name="final_norm",
    )(x, g.reshape(1, k).astype(F32))


def _s5_kernel_matrix_body(x_ref, y_ref, o_ref, *, block):
    m = jnp.dot(x_ref[0], y_ref[0], preferred_element_type=F32, precision=lax.Precision.HIGHEST)
    rows = lax.broadcasted_iota(jnp.int32, m.shape, 0) // SSM_GROUP
    cols = lax.broadcasted_iota(jnp.int32, m.shape, 1) // SSM_GROUP
    o_ref[0] = jnp.where(rows <= cols, m, 0.0).astype(o_ref.dtype)


def s5_kernel_matrix(yt, xt):
    g, lc, p2 = yt.shape
    return pl.pallas_call(
        functools.partial(_s5_kernel_matrix_body, block=SSM_BLOCK),
        grid=(g,),
        in_specs=[pl.BlockSpec((1, lc, p2), lambda i: (i, 0, 0)), pl.BlockSpec((1, p2, lc), lambda i: (i, 0, 0))],
        out_specs=pl.BlockSpec((1, lc, lc), lambda i: (i, 0, 0)),
        out_shape=jax.ShapeDtypeStruct((g, lc, lc), BF16),
        compiler_params=_params("parallel"),
        name="s5_kernel_matrix",
    )(yt, xt)


def _s5_block_input_body(u_ref, bx_ref, z_ref):
    z_ref[0] = jnp.dot(u_ref[0], bx_ref[0], preferred_element_type=F32)


def s5_block_input(ug, bx):
    g, nb, lc = ug.shape
    p2 = bx.shape[2]
    return pl.pallas_call(
        _s5_block_input_body,
        grid=(g,),
        in_specs=[pl.BlockSpec((1, nb, lc), lambda i: (i, 0, 0)), pl.BlockSpec((1, lc, p2), lambda i: (i, 0, 0))],
        out_specs=pl.BlockSpec((1, nb, p2), lambda i: (i, 0, 0)),
        out_shape=jax.ShapeDtypeStruct((g, nb, p2), F32),
        compiler_params=_params("parallel"),
        name="s5_block_input",
    )(ug, bx)


def _s5_scan_body(z_ref, a1_ref, a2_ref, h_ref, state_ref, *, half):
    @pl.when(pl.program_id(0) == 0)
    def _():
        state_ref[...] = jnp.zeros(state_ref.shape, F32)

    a1 = a1_ref[...]
    a2 = a2_ref[...]

    def step(k, state):
        h_ref[k] = state
        return state * a1 + pltpu.roll(state, half, 1) * a2 + z_ref[k]

    state_ref[...] = lax.fori_loop(0, z_ref.shape[0], step, state_ref[...])


def s5_scan(zt, a1, a2, *, tb):
    nb, g, p2 = zt.shape
    return pl.pallas_call(
        functools.partial(_s5_scan_body, half=p2 // 2),
        grid=(nb // tb,),
        in_specs=[
            pl.BlockSpec((tb, g, p2), lambda i: (i, 0, 0)),
            pl.BlockSpec((g, p2), lambda i: (0, 0)),
            pl.BlockSpec((g, p2), lambda i: (0, 0)),
        ],
        out_specs=pl.BlockSpec((tb, g, p2), lambda i: (i, 0, 0)),
        out_shape=jax.ShapeDtypeStruct((nb, g, p2), F32),
        scratch_shapes=[pltpu.VMEM((g, p2), F32)],
        compiler_params=_params("arbitrary"),
        name="s5_scan",
    )(zt, a1, a2)


def _s5_output_body(u_ref, mt_ref, h_ref, cx_ref, d_ref, y_ref):
    u = u_ref[0]
    y = jnp.dot(u, mt_ref[0], preferred_element_type=F32)
    y = y + jnp.dot(h_ref[0], cx_ref[0], preferred_element_type=F32)
    y = y + d_ref[0] * u.astype(F32)
    y_ref[0] = jax.nn.gelu(y).astype(y_ref.dtype)


def s5_output(ug, mt, hg, cx, dg):
    g, nb, lc = ug.shape
    p2 = hg.shape[2]
    return pl.pallas_call(
        _s5_output_body,
        grid=(g,),
        in_specs=[
            pl.BlockSpec((1, nb, lc), lambda i: (i, 0, 0)),
            pl.BlockSpec((1, lc, lc), lambda i: (i, 0, 0)),
            pl.BlockSpec((1, nb, p2), lambda i: (i, 0, 0)),
            pl.BlockSpec((1, p2, lc), lambda i: (i, 0, 0)),
            pl.BlockSpec((1, 1, lc), lambda i: (i, 0, 0)),
        ],
        out_specs=pl.BlockSpec((1, nb, lc), lambda i: (i, 0, 0)),
        out_shape=jax.ShapeDtypeStruct((g, nb, lc), BF16),
        compiler_params=_params("parallel"),
        name="s5_output",
    )(ug, mt, hg, cx, dg)


def _glu_norm_body(y_ref, w_ref, b_ref, g_ref, o_ref):
    y = y_ref[...]
    z = jnp.dot(y, w_ref[...], preferred_element_type=F32) + b_ref[...]
    o = y.astype(F32) * jax.nn.sigmoid(z)
    o_ref[...] = _rms(o, g_ref[...]).astype(o_ref.dtype)


def glu_norm(y, w, b, g, *, tm):
    m, k = y.shape
    return pl.pallas_call(
        _glu_norm_body,
        grid=(m // tm,),
        in_specs=[
            pl.BlockSpec((tm, k), lambda i: (i, 0)),
            pl.BlockSpec((k, k), lambda i: (0, 0)),
            pl.BlockSpec((1, k), lambda i: (0, 0)),
            pl.BlockSpec((1, k), lambda i: (0, 0)),
        ],
        out_specs=pl.BlockSpec((tm, k), lambda i: (i, 0)),
        out_shape=jax.ShapeDtypeStruct((m, k), BF16),
        compiler_params=_params("parallel"),
        name="glu_norm",
    )(y, w, b.reshape(1, k).astype(F32), g.reshape(1, k).astype(F32))


def s5_group(u, a_re, a_im, log_step, b_re, b_im, c_re, c_im, d_skip, w_glu, b_glu, out_g):
    s, d_ssm = u.shape
    g, p = a_re.shape
    c = SSM_GROUP
    blk = SSM_BLOCK
    nb = s // blk
    lc = blk * c

    a = lax.complex(a_re.astype(F32), a_im.astype(F32))
    step = jnp.exp(log_step.astype(F32))[:, None]
    log_abar = a * step
    a_bar = jnp.exp(log_abar)
    b_bar = ((a_bar - 1.0) / a)[..., None] * lax.complex(b_re.astype(F32), b_im.astype(F32))
    cm = lax.complex(c_re.astype(F32), c_im.astype(F32))
    pos = jnp.arange(blk, dtype=F32)

    def apow(e):
        return jnp.exp(log_abar[:, None, :] * e[None, :, None])

    xm = (apow(pos)[:, :, None, :] * cm[:, None, :, :]).reshape(g, lc, p)
    ym = (jnp.swapaxes(apow(-pos), 1, 2)[:, :, :, None] * b_bar[:, :, None, :]).reshape(g, p, lc)
    x_ri = jnp.concatenate([jnp.real(xm), -jnp.imag(xm)], axis=2)
    y_ri = jnp.concatenate([jnp.real(ym), jnp.imag(ym)], axis=1)
    mt = s5_kernel_matrix(jnp.swapaxes(y_ri, 1, 2), jnp.swapaxes(x_ri, 1, 2))

    bxm = (apow(blk - 1.0 - pos)[:, :, None, :] * jnp.swapaxes(b_bar, 1, 2)[:, None, :, :]).reshape(g, lc, p)
    bx = jnp.concatenate([jnp.real(bxm), jnp.imag(bxm)], axis=2).astype(BF16)
    cxm = jnp.swapaxes((apow(pos + 1.0)[:, :, None, :] * cm[:, None, :, :]).reshape(g, lc, p), 1, 2)
    cx = jnp.concatenate([jnp.real(cxm), -jnp.imag(cxm)], axis=1).astype(BF16)
    a_blk = jnp.exp(log_abar * blk)
    a1 = jnp.concatenate([jnp.real(a_blk), jnp.real(a_blk)], axis=1)
    a2 = jnp.concatenate([-jnp.imag(a_blk), jnp.imag(a_blk)], axis=1)
    dg = jnp.tile(d_skip.astype(F32).reshape(g, 1, c), (1, blk, 1)).reshape(g, 1, lc)

    ug = u.reshape(nb, blk, g, c).transpose(2, 0, 1, 3).reshape(g, nb, lc)
    z = s5_block_input(ug, bx)
    h = s5_scan(z.transpose(1, 0, 2), a1, a2, tb=min(64, nb))
    hg = h.transpose(1, 0, 2).astype(BF16)
    yg = s5_output(ug, mt, hg, cx, dg)
    y = yg.reshape(g, nb, blk, c).transpose(1, 2, 0, 3).reshape(s, d_ssm)
    return glu_norm(y, w_glu.astype(BF16), b_glu, out_g, tm=min(512, s))


def _diff_attn_body(lam_ref, q_ref, k_ref, v_ref, g_ref, o_ref, m_scr, l_scr, acc_scr, *, tq, out_scale):
    i = pl.program_id(1)
    q = (q_ref[...].astype(F32) * (ATT_HEAD_DIM ** -0.5 * math.log2(math.e))).astype(q_ref.dtype)
    lane = lax.broadcasted_iota(jnp.int32, q.shape, 1)
    zero = jnp.zeros_like(q)
    qq = jnp.concatenate([jnp.where(lane < ATT_HEAD_DIM, q, zero), jnp.where(lane >= ATT_HEAD_DIM, q, zero)], axis=0)

    m_scr[...] = jnp.full(m_scr.shape, -jnp.inf, F32)
    l_scr[...] = jnp.zeros(l_scr.shape, F32)
    acc_scr[...] = jnp.zeros(acc_scr.shape, F32)
    reps = tq // LANES

    def step(start, masked):
        kb = k_ref[pl.ds(start, tq), :]
        vb = v_ref[pl.ds(start, tq), :]
        s = lax.dot_general(qq, kb, _NT, preferred_element_type=F32)
        if masked:
            qc = (lax.broadcasted_iota(jnp.int32, s.shape, 0) % tq) // CHUNK
            kc = lax.broadcasted_iota(jnp.int32, s.shape, 1) // CHUNK
            s = jnp.where(kc <= qc, s, -jnp.inf)
        m_prev = m_scr[...]
        m_next = jnp.maximum(m_prev, jnp.max(s, axis=1, keepdims=True))
        alpha = jnp.exp2(m_prev - m_next)
        p = jnp.exp2(s - jnp.concatenate([m_next] * reps, axis=1))
        l_scr[...] = alpha * l_scr[...] + jnp.sum(p, axis=1, keepdims=True)
        acc_scr[...] = alpha * acc_scr[...] + jnp.dot(p.astype(vb.dtype), vb, preferred_element_type=F32)
        m_scr[...] = m_next

    def full_pair(j, carry):
        step(pl.multiple_of(2 * j * tq, tq), False)
        step(pl.multiple_of((2 * j + 1) * tq, tq), False)
        return carry

    lax.fori_loop(0, lax.shift_right_logical(i, 1), full_pair, 0)
    diag = pl.multiple_of(i * tq, tq)

    @pl.when((i & 1) == 1)
    def _():
        step(pl.multiple_of((i - 1) * tq, tq), False)
        step(diag, True)

    @pl.when((i & 1) == 0)
    def _():
        step(diag, True)

    o = acc_scr[...] / l_scr[...]
    d = o[:tq] - lam_ref[0] * o[tq:]
    o_ref[...] = (_rms(d, g_ref[...]) * out_scale).astype(o_ref.dtype)


def diff_attention(proj, lam, subln_g, *, n_heads, q_col, k_col, v_col, lam_init, tq):
    s = proj.shape[0]
    dv = ATT_V_DIM
    return pl.pallas_call(
        functools.partial(_diff_attn_body, tq=tq, out_scale=1.0 - lam_init),
        grid=(n_heads, s // tq),
        in_specs=[
            pl.BlockSpec(memory_space=pltpu.SMEM),
            pl.BlockSpec((tq, dv), lambda h, i: (i, q_col // dv + h)),
            pl.BlockSpec((s, dv), lambda h, i: (0, k_col // dv + h)),
            pl.BlockSpec((s, dv), lambda h, i: (0, v_col // dv + h)),
            pl.BlockSpec((1, dv), lambda h, i: (0, 0)),
        ],
        out_specs=pl.BlockSpec((tq, dv), lambda h, i: (i, h)),
        out_shape=jax.ShapeDtypeStruct((s, n_heads * dv), BF16),
        scratch_shapes=[
            pltpu.VMEM((2 * tq, LANES), F32),
            pltpu.VMEM((2 * tq, LANES), F32),
            pltpu.VMEM((2 * tq, dv), F32),
        ],
        compiler_params=_params("parallel", "arbitrary"),
        name="diff_attention",
    )(lam.reshape(1).astype(F32), proj, proj, proj, subln_g.reshape(1, dv).astype(F32))


def _mem_attn_body(q_ref, k_ref, v_ref, o_ref, *, scale):
    s = lax.dot_general(q_ref[...], k_ref[...], _NT, preferred_element_type=F32) * scale
    p = jnp.exp(s - jnp.max(s, axis=1, keepdims=True))
    p = p / jnp.sum(p, axis=1, keepdims=True)
    o_ref[...] = jnp.dot(p.astype(v_ref.dtype), v_ref[...], preferred_element_type=F32).astype(o_ref.dtype)


def mem_attention(q, kv, *, n_heads, tm):
    s, d = q.shape
    m = kv.shape[0]
    dh = d // n_heads
    return pl.pallas_call(
        functools.partial(_mem_attn_body, scale=dh ** -0.5),
        grid=(s // tm, n_heads),
        in_specs=[
            pl.BlockSpec((tm, dh), lambda i, h: (i, h)),
            pl.BlockSpec((m, dh), lambda i, h: (0, h)),
            pl.BlockSpec((m, dh), lambda i, h: (0, n_heads + h)),
        ],
        out_specs=pl.BlockSpec((tm, dh), lambda i, h: (i, h)),
        out_shape=jax.ShapeDtypeStruct((s, d), BF16),
        compiler_params=_params("parallel", "parallel"),
        name="mem_attention",
    )(q, kv, kv)


def _cmpx(xs, i, j):
    a, b = xs[i], xs[j]
    if b is None:
        return
    if a is None:
        xs[i], xs[j] = b, None
        return
    xs[i], xs[j] = jnp.maximum(a, b), jnp.minimum(a, b)


def _bitonic_merge_desc(xs):
    n = len(xs)
    j = n // 2
    while j >= 1:
        for i in range(n):
            if (i ^ j) > i:
                _cmpx(xs, i, i ^ j)
        j //= 2
    return xs


def _sort_desc(xs):
    n = len(xs)
    k = 2
    while k <= n:
        j = k // 2
        while j >= 1:
            for i in range(n):
                l = i ^ j
                if l > i:
                    if (i & k) == 0:
                        _cmpx(xs, i, l)
                    else:
                        _cmpx(xs, l, i)
            j //= 2
        k *= 2
    return xs


def _max_or_none(a, b):
    if a is None:
        return b
    if b is None:
        return a
    return jnp.maximum(a, b)


def _merge_top(a, b):
    n = len(a)
    return _bitonic_merge_desc([_max_or_none(a[i], b[n - 1 - i]) for i in range(n)])


def _top_keys(st):
    n_keys = st.shape[0]
    k = PEER_TOPK
    parts = [st[r * SUBLANES:(r + 1) * SUBLANES, :] for r in range(n_keys // SUBLANES)]
    tops = None
    for base in range(0, len(parts), k):
        grp = _sort_desc(parts[base:base + k])
        tops = grp if tops is None else _merge_top(tops, grp)
    shift = SUBLANES // 2
    while shift >= 1:
        tops = _merge_top(tops, [pltpu.roll(x, shift, 0) for x in tops])
        shift //= 2
    return tops


def _grid_top(v1, v2):
    k = PEER_TOPK
    rows = [[v1[i] + v2[j] for j in range(k // (i + 1))] for i in range(k)]
    pad = lambda xs: xs + [None] * (k - len(xs))
    tops = _merge_top(rows[0], pad(rows[1]))
    rest = [x for r in rows[2:] for x in r]
    for base in range(0, len(rest), k):
        tops = _merge_top(tops, _sort_desc(pad(rest[base:base + k])))
    return tops


def _peer_route_body(q_ref, k1_ref, k2_ref, d1_ref, s2_ref, e1_ref, e2_ref):
    n_keys = k1_ref.shape[1]
    dq = k1_ref.shape[2]
    for h in range(k1_ref.shape[0]):
        q1 = q_ref[:, (2 * h) * dq:(2 * h + 1) * dq]
        q2 = q_ref[:, (2 * h + 1) * dq:(2 * h + 2) * dq]
        s1 = lax.dot_general(k1_ref[h], q1, _NT, preferred_element_type=F32)
        s2 = lax.dot_general(k2_ref[h], q2, _NT, preferred_element_type=F32)
        v1 = _top_keys(s1)
        v2 = _top_keys(s2)
        top = _grid_top(v1, v2)
        z = None
        for c in top:
            e = jnp.exp(c - top[0])
            z = e if z is None else z + e
        bc = lambda x: jnp.broadcast_to(x[0:1, :], (n_keys, x.shape[1]))
        d1_ref[h] = bc(top[PEER_TOPK - 1]) - s1
        s2_ref[h] = s2
        e1_ref[h] = jnp.exp(s1 - bc(v1[0]))
        e2_ref[h] = jnp.exp(s2 - bc(v2[0])) / bc(z)


def peer_route(q, k1, k2, *, tt):
    s, dqt = q.shape
    h, n_keys, dq = k1.shape
    out = jax.ShapeDtypeStruct((h, n_keys, s), F32)
    spec = pl.BlockSpec((h, n_keys, tt), lambda i: (0, 0, i))
    kspec = pl.BlockSpec((h, n_keys, dq), lambda i: (0, 0, 0))
    return pl.pallas_call(
        _peer_route_body,
        grid=(s // tt,),
        in_specs=[pl.BlockSpec((tt, dqt), lambda i: (i, 0)), kspec, kspec],
        out_specs=[spec] * 4,
        out_shape=[out] * 4,
        compiler_params=_params("parallel"),
        name="peer_route",
    )(q, k1, k2)


def _peer_mix_body(n_ref, u_ref, vt_ref, d1_ref, s2_ref, e1_ref, e2_ref, x_ref, o_ref,
                   acc_ref, at0_ref, at1_ref, h0_ref, h1_ref, *, n_blocks):
    i = pl.program_id(0)
    j = pl.program_id(1)
    te, tt = h0_ref.shape
    n_keys = s2_ref.shape[1]
    d = acc_ref.shape[0]
    n_chunks = te // n_keys
    half = tt // 2

    @pl.when((i == 0) & (j == 0))
    def _():
        at0_ref[...] = jnp.zeros(at0_ref.shape, F32)
        at1_ref[...] = jnp.zeros(at1_ref.shape, F32)
        h0_ref[...] = jnp.zeros(h0_ref.shape, h0_ref.dtype)
        h1_ref[...] = jnp.zeros(h1_ref.shape, h1_ref.dtype)

    @pl.when(j == 0)
    def _():
        acc_ref[...] = jnp.zeros(acc_ref.shape, F32)

    def stages(at_w, at_r, h_w, h_r, row0):
        weights_valid = (j >= 1) & (j <= n_blocks)
        kq = d // 4

        def act_piece(t0, r0, k0):
            def run():
                part = lax.dot_general(u_ref[r0:r0 + MXU_DIM, k0:k0 + kq], n_ref[t0:t0 + half, k0:k0 + kq],
                                       _NT, preferred_element_type=F32)
                if k0 == 0:
                    at_w[r0:r0 + MXU_DIM, t0:t0 + half] = part
                else:
                    at_w[r0:r0 + MXU_DIM, t0:t0 + half] += part
            return run

        def val_piece(o0, t0):
            def run():
                acc_ref[o0:o0 + MXU_DIM, t0:t0 + half] += jnp.dot(
                    vt_ref[o0:o0 + MXU_DIM, :], h_r[:, t0:t0 + half], preferred_element_type=F32)
            return run

        def weight_slab(c, k0):
            def run():
                cols = slice(c, c + LANES)
                keys = slice(k0, k0 + SLAB_KEYS)
                w = [jnp.zeros((SLAB_KEYS, LANES), F32) for _ in range(n_chunks)]
                for h in range(s2_ref.shape[0]):
                    s2 = s2_ref[h, keys, cols]
                    e2 = e2_ref[h, keys, cols]
                    for r in range(n_chunks):
                        d1 = d1_ref[h, row0 + r:row0 + r + 1, cols]
                        e1v = e1_ref[h, row0 + r:row0 + r + 1, cols]
                        w[r] = w[r] + jnp.where(s2 >= d1, e2 * e1v, 0.0)
                for r in range(n_chunks):
                    out = slice(r * n_keys + k0, r * n_keys + k0 + SLAB_KEYS)
                    hv = jax.nn.gelu(at_r[out, cols]) * w[r]
                    h_w[out, cols] = jnp.where(weights_valid, hv, 0.0).astype(h_w.dtype)
            return run

        acts = [act_piece(t0, r0, k0) for t0 in range(0, tt, half) for r0 in range(0, te, MXU_DIM)
                for k0 in range(0, d, kq)]
        vals = [val_piece(o0, t0) for o0 in range(0, d, MXU_DIM) for t0 in range(0, tt, half)]
        slabs = [weight_slab(c, k0) for c in range(0, tt, LANES) for k0 in range(0, n_keys, SLAB_KEYS)]
        mxu = [p for pair in zip(acts, vals) for p in pair]
        per = len(mxu) // len(slabs)
        for idx, slab in enumerate(slabs):
            for piece in mxu[idx * per:(idx + 1) * per]:
                piece()
            slab()

    @pl.when(j % 2 == 0)
    def _():
        stages(at0_ref, at1_ref, h1_ref, h0_ref, n_chunks)

    @pl.when(j % 2 == 1)
    def _():
        stages(at1_ref, at0_ref, h0_ref, h1_ref, 0)

    @pl.when(j == pl.num_programs(1) - 1)
    def _():
        o_ref[...] = x_ref[...] + acc_ref[...].T


def peer_mix(n, u_tab, vt_tab, route, x, *, tt, te):
    s, d = n.shape
    e = u_tab.shape[0]
    nb = e // te
    assert vt_tab.shape == (nb, d, te)
    d1, s2, e1, e2 = route
    h, n_keys, _ = s2.shape
    assert 2 * (te // n_keys) == SUBLANES
    clamp = lambda b: jnp.clip(b, 0, nb - 1)
    rspec = pl.BlockSpec((h, n_keys, tt), lambda i, j: (0, 0, i))
    gspec = pl.BlockSpec((h, SUBLANES, tt), lambda i, j: (0, clamp(j - 1) // 2, i))
    return pl.pallas_call(
        functools.partial(_peer_mix_body, n_blocks=nb),
        grid=(s // tt, nb + 2),
        in_specs=[
            pl.BlockSpec((tt, d), lambda i, j: (i, 0)),
            pl.BlockSpec((te, d), lambda i, j: (clamp(j), 0)),
            pl.BlockSpec((None, d, te), lambda i, j: (clamp(j - 2), 0, 0)),
            gspec, rspec, gspec, rspec,
            pl.BlockSpec((tt, d), lambda i, j: (i, 0)),
        ],
        out_specs=pl.BlockSpec((tt, d), lambda i, j: (i, 0)),
        out_shape=jax.ShapeDtypeStruct((s, d), F32),
        scratch_shapes=[
            pltpu.VMEM((d, tt), F32),
            pltpu.VMEM((te, tt), F32), pltpu.VMEM((te, tt), F32),
            pltpu.VMEM((te, tt), BF16), pltpu.VMEM((te, tt), BF16),
        ],
        compiler_params=_params("arbitrary", "arbitrary"),
        name="peer_mix",
    )(n, u_tab, vt_tab, *route, x)


def kernel(x, mem, norm_mix_g, w_in, ssm_a_re, ssm_a_im, ssm_log_step, ssm_b_re, ssm_b_im, ssm_c_re, ssm_c_im, ssm_d, ssm_w_glu, ssm_b_glu, ssm_out_g, att_lq1, att_lk1, att_lq2, att_lk2, att_subln_g, w_out, norm_mem_g, mem_norm_g, mem_w_q, mem_w_k, mem_w_v, mem_w_o, norm_ffn_g, peer_w_q, peer_k1, peer_k2, peer_u, peer_v, final_g):
    bsz, s, d = x.shape
    assert bsz == 1
    depth = w_in.shape[0]
    d_ssm = ssm_d.shape[1]
    d_att = (w_in.shape[2] - d_ssm) // 3
    n_att_heads = d_att // ATT_V_DIM
    tm = min(512, s)
    tp = min(1024, s)
    tn = 2048

    xs = x.reshape(s, d).astype(F32)
    mem2 = mem.reshape(mem.shape[1], d).astype(F32)
    for l in range(depth):
        proj = norm_matmul(xs, norm_mix_g[l], w_in[l].astype(BF16), tm=tp, tn=tn)
        y_ssm = s5_group(proj[:, :d_ssm], ssm_a_re[l], ssm_a_im[l], ssm_log_step[l], ssm_b_re[l], ssm_b_im[l],
                         ssm_c_re[l], ssm_c_im[l], ssm_d[l], ssm_w_glu[l], ssm_b_glu[l], ssm_out_g[l])
        lam_init = 0.8 - 0.6 * math.exp(-0.3 * l)
        lam = (jnp.exp(jnp.sum(att_lq1[l].astype(F32) * att_lk1[l].astype(F32)))
               - jnp.exp(jnp.sum(att_lq2[l].astype(F32) * att_lk2[l].astype(F32))) + lam_init)
        y_att = diff_attention(proj, lam, att_subln_g[l], n_heads=n_att_heads, q_col=d_ssm, k_col=d_ssm + d_att,
                               v_col=d_ssm + 2 * d_att, lam_init=lam_init, tq=tp)
        w_o = w_out[l].astype(BF16)
        xs = matmul_residual([y_ssm, y_att], [w_o[:d_ssm], w_o[d_ssm:]], xs, tm=tm, tn=tn)

        q = norm_matmul(xs, norm_mem_g[l], mem_w_q[l].astype(BF16), tm=tp, tn=tn)
        w_kv = jnp.concatenate([mem_w_k[l], mem_w_v[l]], axis=1).astype(BF16)
        kv = norm_matmul(mem2, mem_norm_g[l], w_kv, tm=mem2.shape[0], tn=tn)
        o = mem_attention(q, kv, n_heads=N_MEM_HEADS, tm=tm)
        xs = matmul_residual([o], [mem_w_o[l].astype(BF16)], xs, tm=tm, tn=tn)

        pq, n = norm_matmul(xs, norm_ffn_g[l], peer_w_q[l].astype(BF16), tm=tp, tn=tn, return_norm=True)
        route = peer_route(pq, peer_k1[l].astype(BF16), peer_k2[l].astype(BF16), tt=min(256, s))
        te = 512
        vt = peer_v[l].astype(BF16).reshape(-1, te, d).transpose(0, 2, 1)
        xs = peer_mix(n, peer_u[l].astype(BF16), vt, route, xs, tt=tm, te=te)
    return final_norm(xs, final_g, tm=tm).reshape(bsz, s, d)
```

```python
import functools
import math

import jax
import jax.numpy as jnp
from jax import lax
from jax.experimental import pallas as pl
from jax.experimental.pallas import tpu as pltpu

F32 = jnp.float32
BF16 = jnp.bfloat16

EPS = 1e-6
LANES = 128
SUBLANES = 8
MXU_DIM = 256
VMEM_LIMIT_BYTES = 56 * 1024 * 1024

CHUNK = 64
SSM_GROUP = 16
SSM_STATE = 64
SSM_BLOCK = 32
ATT_HEAD_DIM = 64
ATT_V_DIM = 2 * ATT_HEAD_DIM
N_MEM_HEADS = 4
PEER_HEADS = 8
PEER_N_KEYS = 128
PEER_D_HALF = 128
PEER_TOPK = 16
SLAB_KEYS = 32

_NT = (((1,), (1,)), ((), ()))


def _params(*sem):
    return pltpu.CompilerParams(dimension_semantics=sem, vmem_limit_bytes=VMEM_LIMIT_BYTES)


def _rms(x, g):
    return x * lax.rsqrt(jnp.mean(x * x, axis=-1, keepdims=True) + EPS) * g


def _norm_matmul_body(x_ref, g_ref, w_ref, o_ref, xn_ref):
    @pl.when(pl.program_id(1) == 0)
    def _():
        xn_ref[...] = _rms(x_ref[...], g_ref[...]).astype(xn_ref.dtype)

    o_ref[...] = jnp.dot(xn_ref[...], w_ref[...], preferred_element_type=F32).astype(o_ref.dtype)


def norm_matmul(x, g, w, *, tm, tn, return_norm=False):
    m, k = x.shape
    n = w.shape[1]
    out_specs = [pl.BlockSpec((tm, tn), lambda i, j: (i, j))]
    out_shape = [jax.ShapeDtypeStruct((m, n), BF16)]
    scratch = [pltpu.VMEM((tm, k), BF16)]
    if return_norm:
        out_specs.append(pl.BlockSpec((tm, k), lambda i, j: (i, 0)))
        out_shape.append(jax.ShapeDtypeStruct((m, k), BF16))
        scratch = []
    res = pl.pallas_call(
        _norm_matmul_body,
        grid=(m // tm, n // tn),
        in_specs=[
            pl.BlockSpec((tm, k), lambda i, j: (i, 0)),
            pl.BlockSpec((1, k), lambda i, j: (0, 0)),
            pl.BlockSpec((k, tn), lambda i, j: (0, j)),
        ],
        out_specs=out_specs,
        out_shape=out_shape,
        scratch_shapes=scratch,
        compiler_params=_params("parallel", "arbitrary"),
        name="norm_matmul",
    )(x, g.reshape(1, k).astype(F32), w)
    return res if return_norm else res[0]


def _matmul_residual_body(*refs, n_terms):
    x_ref, o_ref = refs[2 * n_terms], refs[2 * n_terms + 1]
    acc = x_ref[...]
    for y_ref, w_ref in zip(refs[:n_terms], refs[n_terms:2 * n_terms]):
        acc = acc + jnp.dot(y_ref[...], w_ref[...], preferred_element_type=F32)
    o_ref[...] = acc


def matmul_residual(ys, ws, x, *, tm, tn):
    m, n = x.shape
    in_specs = [pl.BlockSpec((tm, y.shape[1]), lambda i, j: (i, 0)) for y in ys]
    in_specs += [pl.BlockSpec((w.shape[0], tn), lambda i, j: (0, j)) for w in ws]
    in_specs += [pl.BlockSpec((tm, tn), lambda i, j: (i, j))]
    return pl.pallas_call(
        functools.partial(_matmul_residual_body, n_terms=len(ys)),
        grid=(m // tm, n // tn),
        in_specs=in_specs,
        out_specs=pl.BlockSpec((tm, tn), lambda i, j: (i, j)),
        out_shape=jax.ShapeDtypeStruct((m, n), F32),
        compiler_params=_params("parallel", "parallel"),
        name="matmul_residual",
    )(*ys, *ws, x)


def _final_norm_body(x_ref, g_ref, o_ref):
    o_ref[...] = _rms(x_ref[...], g_ref[...])


def final_norm(x, g, *, tm):
    m, k = x.shape
    return pl.pallas_call(
        _final_norm_body,
        grid=(m // tm,),
        in_specs=[pl.BlockSpec((tm, k), lambda i: (i, 0)), pl.BlockSpec((1, k), lambda i: (0, 0))],
        out_specs=pl.BlockSpec((tm, k), lambda i: (i, 0)),
        out_shape=jax.ShapeDtypeStruct((m, k), F32),
        compiler_params=_params("parallel"),
        name="final_norm",
    )(x, g.reshape(1, k).astype(F32))


def _s5_kernel_matrix_body(x_ref, y_ref, o_ref, *, block):
    m = jnp.dot(x_ref[0], y_ref[0], preferred_element_type=F32, precision=lax.Precision.HIGHEST)
    rows = lax.broadcasted_iota(jnp.int32, m.shape, 0) // SSM_GROUP
    cols = lax.broadcasted_iota(jnp.int32, m.shape, 1) // SSM_GROUP
    o_ref[0] = jnp.where(rows <= cols, m, 0.0).astype(o_ref.dtype)


def s5_kernel_matrix(yt, xt):
    g, lc, p2 = yt.shape
    return pl.pallas_call(
        functools.partial(_s5_kernel_matrix_body, block=SSM_BLOCK),
        grid=(g,),
        in_specs=[pl.BlockSpec((1, lc, p2), lambda i: (i, 0, 0)), pl.BlockSpec((1, p2, lc), lambda i: (i, 0, 0))],
        out_specs=pl.BlockSpec((1, lc, lc), lambda i: (i, 0, 0)),
        out_shape=jax.ShapeDtypeStruct((g, lc, lc), BF16),
        compiler_params=_params("parallel"),
        name="s5_kernel_matrix",
    )(yt, xt)


def _s5_block_input_body(u_ref, bx_ref, z_ref):
    z_ref[0] = jnp.dot(u_ref[0], bx_ref[0], preferred_element_type=F32)


def s5_block_input(ug, bx):
    g, nb, lc = ug.shape
    p2 = bx.shape[2]
    return pl.pallas_call(
        _s5_block_input_body,
        grid=(g,),
        in_specs=[pl.BlockSpec((1, nb, lc), lambda i: (i, 0, 0)), pl.BlockSpec((1, lc, p2), lambda i: (i, 0, 0))],
        out_specs=pl.BlockSpec((1, nb, p2), lambda i: (i, 0, 0)),
        out_shape=jax.ShapeDtypeStruct((g, nb, p2), F32),
        compiler_params=_params("parallel"),
        name="s5_block_input",
    )(ug, bx)


def _s5_scan_body(z_ref, a1_ref, a2_ref, h_ref, state_ref, *, half):
    @pl.when(pl.program_id(0) == 0)
    def _():
        state_ref[...] = jnp.zeros(state_ref.shape, F32)

    a1 = a1_ref[...]
    a2 = a2_ref[...]

    def step(k, state):
        h_ref[k] = state
        return state * a1 + pltpu.roll(state, half, 1) * a2 + z_ref[k]

    state_ref[...] = lax.fori_loop(0, z_ref.shape[0], step, state_ref[...])


def s5_scan(zt, a1, a2, *, tb):
    nb, g, p2 = zt.shape
    return pl.pallas_call(
        functools.partial(_s5_scan_body, half=p2 // 2),
        grid=(nb // tb,),
        in_specs=[
            pl.BlockSpec((tb, g, p2), lambda i: (i, 0, 0)),
            pl.BlockSpec((g, p2), lambda i: (0, 0)),
            pl.BlockSpec((g, p2), lambda i: (0, 0)),
        ],
        out_specs=pl.BlockSpec((tb, g, p2), lambda i: (i, 0, 0)),
        out_shape=jax.ShapeDtypeStruct((nb, g, p2), F32),
        scratch_shapes=[pltpu.VMEM((g, p2), F32)],
        compiler_params=_params("arbitrary"),
        name="s5_scan",
    )(zt, a1, a2)


def _s5_output_body(u_ref, mt_ref, h_ref, cx_ref, d_ref, y_ref):
    u = u_ref[0]
    y = jnp.dot(u, mt_ref[0], preferred_element_type=F32)
    y = y + jnp.dot(h_ref[0], cx_ref[0], preferred_element_type=F32)
    y = y + d_ref[0] * u.astype(F32)
    y_ref[0] = jax.nn.gelu(y).astype(y_ref.dtype)


def s5_output(ug, mt, hg, cx, dg):
    g, nb, lc = ug.shape
    p2 = hg.shape[2]
    return pl.pallas_call(
        _s5_output_body,
        grid=(g,),
        in_specs=[
            pl.BlockSpec((1, nb, lc), lambda i: (i, 0, 0)),
            pl.BlockSpec((1, lc, lc), lambda i: (i, 0, 0)),
            pl.BlockSpec((1, nb, p2), lambda i: (i, 0, 0)),
            pl.BlockSpec((1, p2, lc), lambda i: (i, 0, 0)),
            pl.BlockSpec((1, 1, lc), lambda i: (i, 0, 0)),
        ],
        out_specs=pl.BlockSpec((1, nb, lc), lambda i: (i, 0, 0)),
        out_shape=jax.ShapeDtypeStruct((g, nb, lc), BF16),
        compiler_params=_params("parallel"),
        name="s5_output",
    )(ug, mt, hg, cx, dg)


def _glu_norm_body(y_ref, w_ref, b_ref, g_ref, o_ref):
    y = y_ref[...]
    z = jnp.dot(y, w_ref[...], preferred_element_type=F32) + b_ref[...]
    o = y.astype(F32) * jax.nn.sigmoid(z)
    o_ref[...] = _rms(o, g_ref[...]).astype(o_ref.dtype)


def glu_norm(y, w, b, g, *, tm):
    m, k = y.shape
    return pl.pallas_call(
        _glu_norm_body,
        grid=(m // tm,),
        in_specs=[
            pl.BlockSpec((tm, k), lambda i: (i, 0)),
            pl.BlockSpec((k, k), lambda i: (0, 0)),
            pl.BlockSpec((1, k), lambda i: (0, 0)),
            pl.BlockSpec((1, k), lambda i: (0, 0)),
        ],
        out_specs=pl.BlockSpec((tm, k), lambda i: (i, 0)),
        out_shape=jax.ShapeDtypeStruct((m, k), BF16),
        compiler_params=_params("parallel"),
        name="glu_norm",
    )(y, w, b.reshape(1, k).astype(F32), g.reshape(1, k).astype(F32))


def s5_group(u, a_re, a_im, log_step, b_re, b_im, c_re, c_im, d_skip, w_glu, b_glu, out_g):
    s, d_ssm = u.shape
    g, p = a_re.shape
    c = SSM_GROUP
    blk = SSM_BLOCK
    nb = s // blk
    lc = blk * c

    a = lax.complex(a_re.astype(F32), a_im.astype(F32))
    step = jnp.exp(log_step.astype(F32))[:, None]
    log_abar = a * step
    a_bar = jnp.exp(log_abar)
    b_bar = ((a_bar - 1.0) / a)[..., None] * lax.complex(b_re.astype(F32), b_im.astype(F32))
    cm = lax.complex(c_re.astype(F32), c_im.astype(F32))
    pos = jnp.arange(blk, dtype=F32)

    def apow(e):
        return jnp.exp(log_abar[:, None, :] * e[None, :, None])

    xm = (apow(pos)[:, :, None, :] * cm[:, None, :, :]).reshape(g, lc, p)
    ym = (jnp.swapaxes(apow(-pos), 1, 2)[:, :, :, None] * b_bar[:, :, None, :]).reshape(g, p, lc)
    x_ri = jnp.concatenate([jnp.real(xm), -jnp.imag(xm)], axis=2)
    y_ri = jnp.concatenate([jnp.real(ym), jnp.imag(ym)], axis=1)
    mt = s5_kernel_matrix(jnp.swapaxes(y_ri, 1, 2), jnp.swapaxes(x_ri, 1, 2))

    bxm = (apow(blk - 1.0 - pos)[:, :, None, :] * jnp.swapaxes(b_bar, 1, 2)[:, None, :, :]).reshape(g, lc, p)
    bx = jnp.concatenate([jnp.real(bxm), jnp.imag(bxm)], axis=2).astype(BF16)
    cxm = jnp.swapaxes((apow(pos + 1.0)[:, :, None, :] * cm[:, None, :, :]).reshape(g, lc, p), 1, 2)
    cx = jnp.concatenate([jnp.real(cxm), -jnp.imag(cxm)], axis=1).astype(BF16)
    a_blk = jnp.exp(log_abar * blk)
    a1 = jnp.concatenate([jnp.real(a_blk), jnp.real(a_blk)], axis=1)
    a2 = jnp.concatenate([-jnp.imag(a_blk), jnp.imag(a_blk)], axis=1)
    dg = jnp.tile(d_skip.astype(F32).reshape(g, 1, c), (1, blk, 1)).reshape(g, 1, lc)

    ug = u.reshape(nb, blk, g, c).transpose(2, 0, 1, 3).reshape(g, nb, lc)
    z = s5_block_input(ug, bx)
    h = s5_scan(z.transpose(1, 0, 2), a1, a2, tb=min(64, nb))
    hg = h.transpose(1, 0, 2).astype(BF16)
    yg = s5_output(ug, mt, hg, cx, dg)
    y = yg.reshape(g, nb, blk, c).transpose(1, 2, 0, 3).reshape(s, d_ssm)
    return glu_norm(y, w_glu.astype(BF16), b_glu, out_g, tm=min(512, s))


def _diff_attn_body(lam_ref, q_ref, k_ref, v_ref, g_ref, o_ref, m_scr, l_scr, acc_scr, *, tq, out_scale):
    i = pl.program_id(1)
    q = (q_ref[...].astype(F32) * (ATT_HEAD_DIM ** -0.5 * math.log2(math.e))).astype(q_ref.dtype)
    lane = lax.broadcasted_iota(jnp.int32, q.shape, 1)
    zero = jnp.zeros_like(q)
    qq = jnp.concatenate([jnp.where(lane < ATT_HEAD_DIM, q, zero), jnp.where(lane >= ATT_HEAD_DIM, q, zero)], axis=0)

    m_scr[...] = jnp.full(m_scr.shape, -jnp.inf, F32)
    l_scr[...] = jnp.zeros(l_scr.shape, F32)
    acc_scr[...] = jnp.zeros(acc_scr.shape, F32)
    reps = tq // LANES

    def step(start, masked):
        kb = k_ref[pl.ds(start, tq), :]
        vb = v_ref[pl.ds(start, tq), :]
        s = lax.dot_general(qq, kb, _NT, preferred_element_type=F32)
        if masked:
            qc = (lax.broadcasted_iota(jnp.int32, s.shape, 0) % tq) // CHUNK
            kc = lax.broadcasted_iota(jnp.int32, s.shape, 1) // CHUNK
            s = jnp.where(kc <= qc, s, -jnp.inf)
        m_prev = m_scr[...]
        m_next = jnp.maximum(m_prev, jnp.max(s, axis=1, keepdims=True))
        alpha = jnp.exp2(m_prev - m_next)
        p = jnp.exp2(s - jnp.concatenate([m_next] * reps, axis=1))
        l_scr[...] = alpha * l_scr[...] + jnp.sum(p, axis=1, keepdims=True)
        acc_scr[...] = alpha * acc_scr[...] + jnp.dot(p.astype(vb.dtype), vb, preferred_element_type=F32)
        m_scr[...] = m_next

    def full_pair(j, carry):
        step(pl.multiple_of(2 * j * tq, tq), False)
        step(pl.multiple_of((2 * j + 1) * tq, tq), False)
        return carry

    lax.fori_loop(0, lax.shift_right_logical(i, 1), full_pair, 0)
    diag = pl.multiple_of(i * tq, tq)

    @pl.when((i & 1) == 1)
    def _():
        step(pl.multiple_of((i - 1) * tq, tq), False)
        step(diag, True)

    @pl.when((i & 1) == 0)
    def _():
        step(diag, True)

    o = acc_scr[...] / l_scr[...]
    d = o[:tq] - lam_ref[0] * o[tq:]
    o_ref[...] = (_rms(d, g_ref[...]) * out_scale).astype(o_ref.dtype)


def diff_attention(proj, lam, subln_g, *, n_heads, q_col, k_col, v_col, lam_init, tq):
    s = proj.shape[0]
    dv = ATT_V_DIM
    return pl.pallas_call(
        functools.partial(_diff_attn_body, tq=tq, out_scale=1.0 - lam_init),
        grid=(n_heads, s // tq),
        in_specs=[
            pl.BlockSpec(memory_space=pltpu.SMEM),
            pl.BlockSpec((tq, dv), lambda h, i: (i, q_col // dv + h)),
            pl.BlockSpec((s, dv), lambda h, i: (0, k_col // dv + h)),
            pl.BlockSpec((s, dv), lambda h, i: (0, v_col // dv + h)),
            pl.BlockSpec((1, dv), lambda h, i: (0, 0)),
        ],
        out_specs=pl.BlockSpec((tq, dv), lambda h, i: (i, h)),
        out_shape=jax.ShapeDtypeStruct((s, n_heads * dv), BF16),
        scratch_shapes=[
            pltpu.VMEM((2 * tq, LANES), F32),
            pltpu.VMEM((2 * tq, LANES), F32),
            pltpu.VMEM((2 * tq, dv), F32),
        ],
        compiler_params=_params("parallel", "arbitrary"),
        name="diff_attention",
    )(lam.reshape(1).astype(F32), proj, proj, proj, subln_g.reshape(1, dv).astype(F32))


def _mem_attn_body(q_ref, k_ref, v_ref, o_ref, *, scale):
    s = lax.dot_general(q_ref[...], k_ref[...], _NT, preferred_element_type=F32) * scale
    p = jnp.exp(s - jnp.max(s, axis=1, keepdims=True))
    p = p / jnp.sum(p, axis=1, keepdims=True)
    o_ref[...] = jnp.dot(p.astype(v_ref.dtype), v_ref[...], preferred_element_type=F32).astype(o_ref.dtype)


def mem_attention(q, kv, *, n_heads, tm):
    s, d = q.shape
    m = kv.shape[0]
    dh = d // n_heads
    return pl.pallas_call(
        functools.partial(_mem_attn_body, scale=dh ** -0.5),
        grid=(s // tm, n_heads),
        in_specs=[
            pl.BlockSpec((tm, dh), lambda i, h: (i, h)),
            pl.BlockSpec((m, dh), lambda i, h: (0, h)),
            pl.BlockSpec((m, dh), lambda i, h: (0, n_heads + h)),
        ],
        out_specs=pl.BlockSpec((tm, dh), lambda i, h: (i, h)),
        out_shape=jax.ShapeDtypeStruct((s, d), BF16),
        compiler_params=_params("parallel", "parallel"),
        name="mem_attention",
    )(q, kv, kv)


def _cmpx(xs, i, j):
    a, b = xs[i], xs[j]
    if b is None:
        return
    if a is None:
        xs[i], xs[j] = b, None
        return
    xs[i], xs[j] = jnp.maximum(a, b), jnp.minimum(a, b)


def _bitonic_merge_desc(xs):
    n = len(xs)
    j = n // 2
    while j >= 1:
        for i in range(n):
            if (i ^ j) > i:
                _cmpx(xs, i, i ^ j)
        j //= 2
    return xs


def _sort_desc(xs):
    n = len(xs)
    k = 2
    while k <= n:
        j = k // 2
        while j >= 1:
            for i in range(n):
                l = i ^ j
                if l > i:
                    if (i & k) == 0:
                        _cmpx(xs, i, l)
                    else:
                        _cmpx(xs, l, i)
            j //= 2
        k *= 2
    return xs


def _max_or_none(a, b):
    if a is None:
        return b
    if b is None:
        return a
    return jnp.maximum(a, b)


def _merge_top(a, b):
    n = len(a)
    return _bitonic_merge_desc([_max_or_none(a[i], b[n - 1 - i]) for i in range(n)])


def _top_keys(st):
    n_keys = st.shape[0]
    k = PEER_TOPK
    parts = [st[r * SUBLANES:(r + 1) * SUBLANES, :] for r in range(n_keys // SUBLANES)]
    tops = None
    for base in range(0, len(parts), k):
        grp = _sort_desc(parts[base:base + k])
        tops = grp if tops is None else _merge_top(tops, grp)
    shift = SUBLANES // 2
    while shift >= 1:
        tops = _merge_top(tops, [pltpu.roll(x, shift, 0) for x in tops])
        shift //= 2
    return tops


def _grid_top(v1, v2):
    k = PEER_TOPK
    rows = [[v1[i] + v2[j] for j in range(k // (i + 1))] for i in range(k)]
    pad = lambda xs: xs + [None] * (k - len(xs))
    tops = _merge_top(rows[0], pad(rows[1]))
    rest = [x for r in rows[2:] for x in r]
    for base in range(0, len(rest), k):
        tops = _merge_top(tops, _sort_desc(pad(rest[base:base + k])))
    return tops


def _peer_route_body(q_ref, k1_ref, k2_ref, d1_ref, s2_ref, e1_ref, e2_ref):
    n_keys = k1_ref.shape[1]
    dq = k1_ref.shape[2]
    for h in range(k1_ref.shape[0]):
        q1 = q_ref[:, (2 * h) * dq:(2 * h + 1) * dq]
        q2 = q_ref[:, (2 * h + 1) * dq:(2 * h + 2) * dq]
        s1 = lax.dot_general(k1_ref[h], q1, _NT, preferred_element_type=F32)
        s2 = lax.dot_general(k2_ref[h], q2, _NT, preferred_element_type=F32)
        v1 = _top_keys(s1)
        v2 = _top_keys(s2)
        top = _grid_top(v1, v2)
        z = None
        for c in top:
            e = jnp.exp(c - top[0])
            z = e if z is None else z + e
        bc = lambda x: jnp.broadcast_to(x[0:1, :], (n_keys, x.shape[1]))
        d1_ref[h] = bc(top[PEER_TOPK - 1]) - s1
        s2_ref[h] = s2
        e1_ref[h] = jnp.exp(s1 - bc(v1[0]))
        e2_ref[h] = jnp.exp(s2 - bc(v2[0])) / bc(z)


def peer_route(q, k1, k2, *, tt):
    s, dqt = q.shape
    h, n_keys, dq = k1.shape
    out = jax.ShapeDtypeStruct((h, n_keys, s), F32)
    spec = pl.BlockSpec((h, n_keys, tt), lambda i: (0, 0, i))
    kspec = pl.BlockSpec((h, n_keys, dq), lambda i: (0, 0, 0))
    return pl.pallas_call(
        _peer_route_body,
        grid=(s // tt,),
        in_specs=[pl.BlockSpec((tt, dqt), lambda i: (i, 0)), kspec, kspec],
        out_specs=[spec] * 4,
        out_shape=[out] * 4,
        compiler_params=_params("parallel"),
        name="peer_route",
    )(q, k1, k2)


def _peer_mix_body(n_ref, u_ref, vt_ref, d1_ref, s2_ref, e1_ref, e2_ref, x_ref, o_ref,
                   acc_ref, at0_ref, at1_ref, h0_ref, h1_ref, *, n_blocks):
    i = pl.program_id(0)
    j = pl.program_id(1)
    te, tt = h0_ref.shape
    n_keys = s2_ref.shape[1]
    d = acc_ref.shape[0]
    n_chunks = te // n_keys
    half = tt // 2

    @pl.when((i == 0) & (j == 0))
    def _():
        at0_ref[...] = jnp.zeros(at0_ref.shape, F32)
        at1_ref[...] = jnp.zeros(at1_ref.shape, F32)
        h0_ref[...] = jnp.zeros(h0_ref.shape, h0_ref.dtype)
        h1_ref[...] = jnp.zeros(h1_ref.shape, h1_ref.dtype)

    @pl.when(j == 0)
    def _():
        acc_ref[...] = jnp.zeros(acc_ref.shape, F32)

    def stages(at_w, at_r, h_w, h_r, row0):
        weights_valid = (j >= 1) & (j <= n_blocks)
        kq = d // 2
        val_rows = 2 * MXU_DIM

        def act_piece(t0, r0, k0):
            def run():
                part = lax.dot_general(u_ref[r0:r0 + MXU_DIM, k0:k0 + kq], n_ref[t0:t0 + half, k0:k0 + kq],
                                       _NT, preferred_element_type=F32)
                if k0 == 0:
                    at_w[r0:r0 + MXU_DIM, t0:t0 + half] = part
                else:
                    at_w[r0:r0 + MXU_DIM, t0:t0 + half] += part
            return run

        def val_piece(o0, t0):
            def run():
                acc_ref[o0:o0 + val_rows, t0:t0 + half] += jnp.dot(
                    vt_ref[o0:o0 + val_rows, :], h_r[:, t0:t0 + half], preferred_element_type=F32)
            return run

        def weight_slab(c, k0):
            def run():
                cols = slice(c, c + LANES)
                keys = slice(k0, k0 + SLAB_KEYS)
                w = [jnp.zeros((SLAB_KEYS, LANES), F32) for _ in range(n_chunks)]
                for h in range(s2_ref.shape[0]):
                    s2 = s2_ref[h, keys, cols]
                    e2 = e2_ref[h, keys, cols]
                    for r in range(n_chunks):
                        d1 = d1_ref[h, row0 + r:row0 + r + 1, cols]
                        e1v = e1_ref[h, row0 + r:row0 + r + 1, cols]
                        w[r] = w[r] + jnp.where(s2 >= d1, e2 * e1v, 0.0)
                for r in range(n_chunks):
                    out = slice(r * n_keys + k0, r * n_keys + k0 + SLAB_KEYS)
                    hv = jax.nn.gelu(at_r[out, cols]) * w[r]
                    h_w[out, cols] = jnp.where(weights_valid, hv, 0.0).astype(h_w.dtype)
            return run

        acts = [act_piece(t0, r0, k0) for t0 in range(0, tt, half) for r0 in range(0, te, MXU_DIM)
                for k0 in range(0, d, kq)]
        vals = [val_piece(o0, t0) for o0 in range(0, d, val_rows) for t0 in range(0, tt, half)]
        slabs = [weight_slab(c, k0) for c in range(0, tt, LANES) for k0 in range(0, n_keys, SLAB_KEYS)]
        ratio = len(vals) // len(acts)
        mxu = [p for a in range(len(acts)) for p in [acts[a]] + vals[a * ratio:(a + 1) * ratio]]
        assert len(mxu) == len(acts) + len(vals)
        for idx, slab in enumerate(slabs):
            for piece in mxu[idx * len(mxu) // len(slabs):(idx + 1) * len(mxu) // len(slabs)]:
                piece()
            slab()

    @pl.when(j % 2 == 0)
    def _():
        stages(at0_ref, at1_ref, h1_ref, h0_ref, n_chunks)

    @pl.when(j % 2 == 1)
    def _():
        stages(at1_ref, at0_ref, h0_ref, h1_ref, 0)

    @pl.when(j == pl.num_programs(1) - 1)
    def _():
        o_ref[...] = x_ref[...] + acc_ref[...].T


def peer_mix(n, u_tab, vt_tab, route, x, *, tt, te):
    s, d = n.shape
    e = u_tab.shape[0]
    nb = e // te
    assert vt_tab.shape == (nb, d, te)
    d1, s2, e1, e2 = route
    h, n_keys, _ = s2.shape
    assert 2 * (te // n_keys) == SUBLANES
    clamp = lambda b: jnp.clip(b, 0, nb - 1)
    rspec = pl.BlockSpec((h, n_keys, tt), lambda i, j: (0, 0, i))
    gspec = pl.BlockSpec((h, SUBLANES, tt), lambda i, j: (0, clamp(j - 1) // 2, i))
    return pl.pallas_call(
        functools.partial(_peer_mix_body, n_blocks=nb),
        grid=(s // tt, nb + 2),
        in_specs=[
            pl.BlockSpec((tt, d), lambda i, j: (i, 0)),
            pl.BlockSpec((te, d), lambda i, j: (clamp(j), 0)),
            pl.BlockSpec((None, d, te), lambda i, j: (clamp(j - 2), 0, 0)),
            gspec, rspec, gspec, rspec,
            pl.BlockSpec((tt, d), lambda i, j: (i, 0)),
        ],
        out_specs=pl.BlockSpec((tt, d), lambda i, j: (i, 0)),
        out_shape=jax.ShapeDtypeStruct((s, d), F32),
        scratch_shapes=[
            pltpu.VMEM((d, tt), F32),
            pltpu.VMEM((te, tt), F32), pltpu.VMEM((te, tt), F32),
            pltpu.VMEM((te, tt), BF16), pltpu.VMEM((te, tt), BF16),
        ],
        compiler_params=_params("arbitrary", "arbitrary"),
        name="peer_mix",
    )(n, u_tab, vt_tab, *route, x)


def kernel(x, mem, norm_mix_g, w_in, ssm_a_re, ssm_a_im, ssm_log_step, ssm_b_re, ssm_b_im, ssm_c_re, ssm_c_im, ssm_d, ssm_w_glu, ssm_b_glu, ssm_out_g, att_lq1, att_lk1, att_lq2, att_lk2, att_subln_g, w_out, norm_mem_g, mem_norm_g, mem_w_q, mem_w_k, mem_w_v, mem_w_o, norm_ffn_g, peer_w_q, peer_k1, peer_k2, peer_u, peer_v, final_g):
    bsz, s, d = x.shape
    assert bsz == 1
    depth = w_in.shape[0]
    d_ssm = ssm_d.shape[1]
    d_att = (w_in.shape[2] - d_ssm) // 3
    n_att_heads = d_att // ATT_V_DIM
    tm = min(512, s)
    tp = min(1024, s)
    tn = 2048

    xs = x.reshape(s, d).astype(F32)
    mem2 = mem.reshape(mem.shape[1], d).astype(F32)
    for l in range(depth):
        proj = norm_matmul(xs, norm_mix_g[l], w_in[l].astype(BF16), tm=tp, tn=tn)
        y_ssm = s5_group(proj[:, :d_ssm], ssm_a_re[l], ssm_a_im[l], ssm_log_step[l], ssm_b_re[l], ssm_b_im[l],
                         ssm_c_re[l], ssm_c_im[l], ssm_d[l], ssm_w_glu[l], ssm_b_glu[l], ssm_out_g[l])
        lam_init = 0.8 - 0.6 * math.exp(-0.3 * l)
        lam = (jnp.exp(jnp.sum(att_lq1[l].astype(F32) * att_lk1[l].astype(F32)))
               - jnp.exp(jnp.sum(att_lq2[l].astype(F32) * att_lk2[l].astype(F32))) + lam_init)
        y_att = diff_attention(proj, lam, att_subln_g[l], n_heads=n_att_heads, q_col=d_ssm, k_col=d_ssm + d_att,
                               v_col=d_ssm + 2 * d_att, lam_init=lam_init, tq=tp)
        w_o = w_out[l].astype(BF16)
        xs = matmul_residual([y_ssm, y_att], [w_o[:d_ssm], w_o[d_ssm:]], xs, tm=tm, tn=tn)

        q = norm_matmul(xs, norm_mem_g[l], mem_w_q[l].astype(BF16), tm=tp, tn=tn)
        w_kv = jnp.concatenate([mem_w_k[l], mem_w_v[l]], axis=1).astype(BF16)
        kv = norm_matmul(mem2, mem_norm_g[l], w_kv, tm=mem2.shape[0], tn=tn)
        o = mem_attention(q, kv, n_heads=N_MEM_HEADS, tm=tm)
        xs = matmul_residual([o], [mem_w_o[l].astype(BF16)], xs, tm=tm, tn=tn)

        pq, n = norm_matmul(xs, norm_ffn_g[l], peer_w_q[l].astype(BF16), tm=tp, tn=tn, return_norm=True)
        route = peer_route(pq, peer_k1[l].astype(BF16), peer_k2[l].astype(BF16), tt=min(256, s))
        te = 512
        vt = peer_v[l].astype(BF16).reshape(-1, te, d).transpose(0, 2, 1)
        xs = peer_mix(n, peer_u[l].astype(BF16), vt, route, xs, tt=tm, te=te)
    return final_norm(xs, final_g, tm=tm).reshape(bsz, s, d)
```

```python
import functools
import math

import jax
import jax.numpy as jnp
from jax import lax
from jax.experimental import pallas as pl
from jax.experimental.pallas import tpu as pltpu

F32 = jnp.float32
BF16 = jnp.bfloat16

EPS = 1e-6
LANES = 128
SUBLANES = 8
MXU_DIM = 256
VMEM_LIMIT_BYTES = 56 * 1024 * 1024

CHUNK = 64
SSM_GROUP = 16
SSM_STATE = 64
SSM_BLOCK = 32
ATT_HEAD_DIM = 64
ATT_V_DIM = 2 * ATT_HEAD_DIM
N_MEM_HEADS = 4
PEER_HEADS = 8
PEER_N_KEYS = 128
PEER_D_HALF = 128
PEER_TOPK = 16
SLAB_KEYS = 32

_NT = (((1,), (1,)), ((), ()))


def _params(*sem):
    return pltpu.CompilerParams(dimension_semantics=sem, vmem_limit_bytes=VMEM_LIMIT_BYTES)


def _rms(x, g):
    return x * lax.rsqrt(jnp.mean(x * x, axis=-1, keepdims=True) + EPS) * g


def _norm_matmul_body(x_ref, g_ref, w_ref, o_ref, xn_ref):
    @pl.when(pl.program_id(1) == 0)
    def _():
        xn_ref[...] = _rms(x_ref[...], g_ref[...]).astype(xn_ref.dtype)

    o_ref[...] = jnp.dot(xn_ref[...], w_ref[...], preferred_element_type=F32).astype(o_ref.dtype)


def norm_matmul(x, g, w, *, tm, tn, return_norm=False):
    m, k = x.shape
    n = w.shape[1]
    out_specs = [pl.BlockSpec((tm, tn), lambda i, j: (i, j))]
    out_shape = [jax.ShapeDtypeStruct((m, n), BF16)]
    scratch = [pltpu.VMEM((tm, k), BF16)]
    if return_norm:
        out_specs.append(pl.BlockSpec((tm, k), lambda i, j: (i, 0)))
        out_shape.append(jax.ShapeDtypeStruct((m, k), BF16))
        scratch = []
    res = pl.pallas_call(
        _norm_matmul_body,
        grid=(m // tm, n // tn),
        in_specs=[
            pl.BlockSpec((tm, k), lambda i, j: (i, 0)),
            pl.BlockSpec((1, k), lambda i, j: (0, 0)),
            pl.BlockSpec((k, tn), lambda i, j: (0, j)),
        ],
        out_specs=out_specs,
        out_shape=out_shape,
        scratch_shapes=scratch,
        compiler_params=_params("parallel", "arbitrary"),
        name="norm_matmul",
    )(x, g.reshape(1, k).astype(F32), w)
    return res if return_norm else res[0]


def _matmul_residual_body(*refs, n_terms):
    x_ref, o_ref = refs[2 * n_terms], refs[2 * n_terms + 1]
    acc = x_ref[...]
    for y_ref, w_ref in zip(refs[:n_terms], refs[n_terms:2 * n_terms]):
        acc = acc + jnp.dot(y_ref[...], w_ref[...], preferred_element_type=F32)
    o_ref[...] = acc


def matmul_residual(ys, ws, x, *, tm, tn):
    m, n = x.shape
    in_specs = [pl.BlockSpec((tm, y.shape[1]), lambda i, j: (i, 0)) for y in ys]
    in_specs += [pl.BlockSpec((w.shape[0], tn), lambda i, j: (0, j)) for w in ws]
    in_specs += [pl.BlockSpec((tm, tn), lambda i, j: (i, j))]
    return pl.pallas_call(
        functools.partial(_matmul_residual_body, n_terms=len(ys)),
        grid=(m // tm, n // tn),
        in_specs=in_specs,
        out_specs=pl.BlockSpec((tm, tn), lambda i, j: (i, j)),
        out_shape=jax.ShapeDtypeStruct((m, n), F32),
        compiler_params=_params("parallel", "parallel"),
        name="matmul_residual",
    )(*ys, *ws, x)


def _final_norm_body(x_ref, g_ref, o_ref):
    o_ref[...] = _rms(x_ref[...], g_ref[...])


def final_norm(x, g, *, tm):
    m, k = x.shape
    return pl.pallas_call(
        _final_norm_body,
        grid=(m // tm,),
        in_specs=[pl.BlockSpec((tm, k), lambda i: (i, 0)), pl.BlockSpec((1, k), lambda i: (0, 0))],
        out_specs=pl.BlockSpec((tm, k), lambda i: (i, 0)),
        out_shape=jax.ShapeDtypeStruct((m, k), F32),
        compiler_params=_params("parallel"),
        name="final_norm",
    )(x, g.reshape(1, k).astype(F32))


def _s5_kernel_matrix_body(x_ref, y_ref, o_ref, *, block):
    m = jnp.dot(x_ref[0], y_ref[0], preferred_element_type=F32, precision=lax.Precision.HIGHEST)
    rows = lax.broadcasted_iota(jnp.int32, m.shape, 0) // SSM_GROUP
    cols = lax.broadcasted_iota(jnp.int32, m.shape, 1) // SSM_GROUP
    o_ref[0] = jnp.where(rows <= cols, m, 0.0).astype(o_ref.dtype)


def s5_kernel_matrix(yt, xt):
    g, lc, p2 = yt.shape
    return pl.pallas_call(
        functools.partial(_s5_kernel_matrix_body, block=SSM_BLOCK),
        grid=(g,),
        in_specs=[pl.BlockSpec((1, lc, p2), lambda i: (i, 0, 0)), pl.BlockSpec((1, p2, lc), lambda i: (i, 0, 0))],
        out_specs=pl.BlockSpec((1, lc, lc), lambda i: (i, 0, 0)),
        out_shape=jax.ShapeDtypeStruct((g, lc, lc), BF16),
        compiler_params=_params("parallel"),
        name="s5_kernel_matrix",
    )(yt, xt)


def _s5_block_input_body(u_ref, bx_ref, z_ref):
    z_ref[0] = jnp.dot(u_ref[0], bx_ref[0], preferred_element_type=F32)


def s5_block_input(ug, bx):
    g, nb, lc = ug.shape
    p2 = bx.shape[2]
    return pl.pallas_call(
        _s5_block_input_body,
        grid=(g,),
        in_specs=[pl.BlockSpec((1, nb, lc), lambda i: (i, 0, 0)), pl.BlockSpec((1, lc, p2), lambda i: (i, 0, 0))],
        out_specs=pl.BlockSpec((1, nb, p2), lambda i: (i, 0, 0)),
        out_shape=jax.ShapeDtypeStruct((g, nb, p2), F32),
        compiler_params=_params("parallel"),
        name="s5_block_input",
    )(ug, bx)


def _s5_scan_body(z_ref, a1_ref, a2_ref, h_ref, state_ref, *, half):
    @pl.when(pl.program_id(0) == 0)
    def _():
        state_ref[...] = jnp.zeros(state_ref.shape, F32)

    a1 = a1_ref[...]
    a2 = a2_ref[...]

    def step(k, state):
        h_ref[k] = state
        return state * a1 + pltpu.roll(state, half, 1) * a2 + z_ref[k]

    state_ref[...] = lax.fori_loop(0, z_ref.shape[0], step, state_ref[...])


def s5_scan(zt, a1, a2, *, tb):
    nb, g, p2 = zt.shape
    return pl.pallas_call(
        functools.partial(_s5_scan_body, half=p2 // 2),
        grid=(nb // tb,),
        in_specs=[
            pl.BlockSpec((tb, g, p2), lambda i: (i, 0, 0)),
            pl.BlockSpec((g, p2), lambda i: (0, 0)),
            pl.BlockSpec((g, p2), lambda i: (0, 0)),
        ],
        out_specs=pl.BlockSpec((tb, g, p2), lambda i: (i, 0, 0)),
        out_shape=jax.ShapeDtypeStruct((nb, g, p2), F32),
        scratch_shapes=[pltpu.VMEM((g, p2), F32)],
        compiler_params=_params("arbitrary"),
        name="s5_scan",
    )(zt, a1, a2)


def _s5_output_body(u_ref, mt_ref, h_ref, cx_ref, d_ref, y_ref):
    u = u_ref[0]
    y = jnp.dot(u, mt_ref[0], preferred_element_type=F32)
    y = y + jnp.dot(h_ref[0], cx_ref[0], preferred_element_type=F32)
    y = y + d_ref[0] * u.astype(F32)
    y_ref[0] = jax.nn.gelu(y).astype(y_ref.dtype)


def s5_output(ug, mt, hg, cx, dg):
    g, nb, lc = ug.shape
    p2 = hg.shape[2]
    return pl.pallas_call(
        _s5_output_body,
        grid=(g,),
        in_specs=[
            pl.BlockSpec((1, nb, lc), lambda i: (i, 0, 0)),
            pl.BlockSpec((1, lc, lc), lambda i: (i, 0, 0)),
            pl.BlockSpec((1, nb, p2), lambda i: (i, 0, 0)),
            pl.BlockSpec((1, p2, lc), lambda i: (i, 0, 0)),
            pl.BlockSpec((1, 1, lc), lambda i: (i, 0, 0)),
        ],
        out_specs=pl.BlockSpec((1, nb, lc), lambda i: (i, 0, 0)),
        out_shape=jax.ShapeDtypeStruct((g, nb, lc), BF16),
        compiler_params=_params("parallel"),
        name="s5_output",
    )(ug, mt, hg, cx, dg)


def _glu_norm_body(y_ref, w_ref, b_ref, g_ref, o_ref):
    y = y_ref[...]
    z = jnp.dot(y, w_ref[...], preferred_element_type=F32) + b_ref[...]
    o = y.astype(F32) * jax.nn.sigmoid(z)
    o_ref[...] = _rms(o, g_ref[...]).astype(o_ref.dtype)


def glu_norm(y, w, b, g, *, tm):
    m, k = y.shape
    return pl.pallas_call(
        _glu_norm_body,
        grid=(m // tm,),
        in_specs=[
            pl.BlockSpec((tm, k), lambda i: (i, 0)),
            pl.BlockSpec((k, k), lambda i: (0, 0)),
            pl.BlockSpec((1, k), lambda i: (0, 0)),
            pl.BlockSpec((1, k), lambda i: (0, 0)),
        ],
        out_specs=pl.BlockSpec((tm, k), lambda i: (i, 0)),
        out_shape=jax.ShapeDtypeStruct((m, k), BF16),
        compiler_params=_params("parallel"),
        name="glu_norm",
    )(y, w, b.reshape(1, k).astype(F32), g.reshape(1, k).astype(F32))


def s5_group(u, a_re, a_im, log_step, b_re, b_im, c_re, c_im, d_skip, w_glu, b_glu, out_g):
    s, d_ssm = u.shape
    g, p = a_re.shape
    c = SSM_GROUP
    blk = SSM_BLOCK
    nb = s // blk
    lc = blk * c

    a = lax.complex(a_re.astype(F32), a_im.astype(F32))
    step = jnp.exp(log_step.astype(F32))[:, None]
    log_abar = a * step
    a_bar = jnp.exp(log_abar)
    b_bar = ((a_bar - 1.0) / a)[..., None] * lax.complex(b_re.astype(F32), b_im.astype(F32))
    cm = lax.complex(c_re.astype(F32), c_im.astype(F32))
    pos = jnp.arange(blk, dtype=F32)

    def apow(e):
        return jnp.exp(log_abar[:, None, :] * e[None, :, None])

    xm = (apow(pos)[:, :, None, :] * cm[:, None, :, :]).reshape(g, lc, p)
    ym = (jnp.swapaxes(apow(-pos), 1, 2)[:, :, :, None] * b_bar[:, :, None, :]).reshape(g, p, lc)
    x_ri = jnp.concatenate([jnp.real(xm), -jnp.imag(xm)], axis=2)
    y_ri = jnp.concatenate([jnp.real(ym), jnp.imag(ym)], axis=1)
    mt = s5_kernel_matrix(jnp.swapaxes(y_ri, 1, 2), jnp.swapaxes(x_ri, 1, 2))

    bxm = (apow(blk - 1.0 - pos)[:, :, None, :] * jnp.swapaxes(b_bar, 1, 2)[:, None, :, :]).reshape(g, lc, p)
    bx = jnp.concatenate([jnp.real(bxm), jnp.imag(bxm)], axis=2).astype(BF16)
    cxm = jnp.swapaxes((apow(pos + 1.0)[:, :, None, :] * cm[:, None, :, :]).reshape(g, lc, p), 1, 2)
    cx = jnp.concatenate([jnp.real(cxm), -jnp.imag(cxm)], axis=1).astype(BF16)
    a_blk = jnp.exp(log_abar * blk)
    a1 = jnp.concatenate([jnp.real(a_blk), jnp.real(a_blk)], axis=1)
    a2 = jnp.concatenate([-jnp.imag(a_blk), jnp.imag(a_blk)], axis=1)
    dg = jnp.tile(d_skip.astype(F32).reshape(g, 1, c), (1, blk, 1)).reshape(g, 1, lc)

    ug = u.reshape(nb, blk, g, c).transpose(2, 0, 1, 3).reshape(g, nb, lc)
    z = s5_block_input(ug, bx)
    h = s5_scan(z.transpose(1, 0, 2), a1, a2, tb=min(64, nb))
    hg = h.transpose(1, 0, 2).astype(BF16)
    yg = s5_output(ug, mt, hg, cx, dg)
    y = yg.reshape(g, nb, blk, c).transpose(1, 2, 0, 3).reshape(s, d_ssm)
    return glu_norm(y, w_glu.astype(BF16), b_glu, out_g, tm=min(512, s))


def _diff_attn_body(lam_ref, q_ref, k_ref, v_ref, g_ref, o_ref, m_scr, l_scr, acc_scr, *, tq, out_scale):
    i = pl.program_id(1)
    q = (q_ref[...].astype(F32) * (ATT_HEAD_DIM ** -0.5 * math.log2(math.e))).astype(q_ref.dtype)
    lane = lax.broadcasted_iota(jnp.int32, q.shape, 1)
    zero = jnp.zeros_like(q)
    qq = jnp.concatenate([jnp.where(lane < ATT_HEAD_DIM, q, zero), jnp.where(lane >= ATT_HEAD_DIM, q, zero)], axis=0)

    m_scr[...] = jnp.full(m_scr.shape, -jnp.inf, F32)
    l_scr[...] = jnp.zeros(l_scr.shape, F32)
    acc_scr[...] = jnp.zeros(acc_scr.shape, F32)
    reps = tq // LANES

    def step(start, masked):
        kb = k_ref[pl.ds(start, tq), :]
        vb = v_ref[pl.ds(start, tq), :]
        s = lax.dot_general(qq, kb, _NT, preferred_element_type=F32)
        if masked:
            qc = (lax.broadcasted_iota(jnp.int32, s.shape, 0) % tq) // CHUNK
            kc = lax.broadcasted_iota(jnp.int32, s.shape, 1) // CHUNK
            s = jnp.where(kc <= qc, s, -jnp.inf)
        m_prev = m_scr[...]
        m_next = jnp.maximum(m_prev, jnp.max(s, axis=1, keepdims=True))
        alpha = jnp.exp2(m_prev - m_next)
        p = jnp.exp2(s - jnp.concatenate([m_next] * reps, axis=1))
        l_scr[...] = alpha * l_scr[...] + jnp.sum(p, axis=1, keepdims=True)
        acc_scr[...] = alpha * acc_scr[...] + jnp.dot(p.astype(vb.dtype), vb, preferred_element_type=F32)
        m_scr[...] = m_next

    def full_pair(j, carry):
        step(pl.multiple_of(2 * j * tq, tq), False)
        step(pl.multiple_of((2 * j + 1) * tq, tq), False)
        return carry

    lax.fori_loop(0, lax.shift_right_logical(i, 1), full_pair, 0)
    diag = pl.multiple_of(i * tq, tq)

    @pl.when((i & 1) == 1)
    def _():
        step(pl.multiple_of((i - 1) * tq, tq), False)
        step(diag, True)

    @pl.when((i & 1) == 0)
    def _():
        step(diag, True)

    o = acc_scr[...] / l_scr[...]
    d = o[:tq] - lam_ref[0] * o[tq:]
    o_ref[...] = (_rms(d, g_ref[...]) * out_scale).astype(o_ref.dtype)


def diff_attention(proj, lam, subln_g, *, n_heads, q_col, k_col, v_col, lam_init, tq):
    s = proj.shape[0]
    dv = ATT_V_DIM
    return pl.pallas_call(
        functools.partial(_diff_attn_body, tq=tq, out_scale=1.0 - lam_init),
        grid=(n_heads, s // tq),
        in_specs=[
            pl.BlockSpec(memory_space=pltpu.SMEM),
            pl.BlockSpec((tq, dv), lambda h, i: (i, q_col // dv + h)),
            pl.BlockSpec((s, dv), lambda h, i: (0, k_col // dv + h)),
            pl.BlockSpec((s, dv), lambda h, i: (0, v_col // dv + h)),
            pl.BlockSpec((1, dv), lambda h, i: (0, 0)),
        ],
        out_specs=pl.BlockSpec((tq, dv), lambda h, i: (i, h)),
        out_shape=jax.ShapeDtypeStruct((s, n_heads * dv), BF16),
        scratch_shapes=[
            pltpu.VMEM((2 * tq, LANES), F32),
            pltpu.VMEM((2 * tq, LANES), F32),
            pltpu.VMEM((2 * tq, dv), F32),
        ],
        compiler_params=_params("parallel", "arbitrary"),
        name="diff_attention",
    )(lam.reshape(1).astype(F32), proj, proj, proj, subln_g.reshape(1, dv).astype(F32))


def _mem_attn_body(q_ref, k_ref, v_ref, o_ref, *, scale):
    s = lax.dot_general(q_ref[...], k_ref[...], _NT, preferred_element_type=F32) * scale
    p = jnp.exp(s - jnp.max(s, axis=1, keepdims=True))
    p = p / jnp.sum(p, axis=1, keepdims=True)
    o_ref[...] = jnp.dot(p.astype(v_ref.dtype), v_ref[...], preferred_element_type=F32).astype(o_ref.dtype)


def mem_attention(q, kv, *, n_heads, tm):
    s, d = q.shape
    m = kv.shape[0]
    dh = d // n_heads
    return pl.pallas_call(
        functools.partial(_mem_attn_body, scale=dh ** -0.5),
        grid=(s // tm, n_heads),
        in_specs=[
            pl.BlockSpec((tm, dh), lambda i, h: (i, h)),
            pl.BlockSpec((m, dh), lambda i, h: (0, h)),
            pl.BlockSpec((m, dh), lambda i, h: (0, n_heads + h)),
        ],
        out_specs=pl.BlockSpec((tm, dh), lambda i, h: (i, h)),
        out_shape=jax.ShapeDtypeStruct((s, d), BF16),
        compiler_params=_params("parallel", "parallel"),
        name="mem_attention",
    )(q, kv, kv)


def _cmpx(xs, i, j):
    a, b = xs[i], xs[j]
    if b is None:
        return
    if a is None:
        xs[i], xs[j] = b, None
        return
    xs[i], xs[j] = jnp.maximum(a, b), jnp.minimum(a, b)


def _bitonic_merge_desc(xs):
    n = len(xs)
    j = n // 2
    while j >= 1:
        for i in range(n):
            if (i ^ j) > i:
                _cmpx(xs, i, i ^ j)
        j //= 2
    return xs


def _sort_desc(xs):
    n = len(xs)
    k = 2
    while k <= n:
        j = k // 2
        while j >= 1:
            for i in range(n):
                l = i ^ j
                if l > i:
                    if (i & k) == 0:
                        _cmpx(xs, i, l)
                    else:
                        _cmpx(xs, l, i)
            j //= 2
        k *= 2
    return xs


def _max_or_none(a, b):
    if a is None:
        return b
    if b is None:
        return a
    return jnp.maximum(a, b)


def _merge_top(a, b):
    n = len(a)
    return _bitonic_merge_desc([_max_or_none(a[i], b[n - 1 - i]) for i in range(n)])


def _top_keys(st):
    n_keys = st.shape[0]
    k = PEER_TOPK
    parts = [st[r * SUBLANES:(r + 1) * SUBLANES, :] for r in range(n_keys // SUBLANES)]
    tops = None
    for base in range(0, len(parts), k):
        grp = _sort_desc(parts[base:base + k])
        tops = grp if tops is None else _merge_top(tops, grp)
    shift = SUBLANES // 2
    while shift >= 1:
        tops = _merge_top(tops, [pltpu.roll(x, shift, 0) for x in tops])
        shift //= 2
    return tops


def _grid_top(v1, v2):
    k = PEER_TOPK
    rows = [[v1[i] + v2[j] for j in range(k // (i + 1))] for i in range(k)]
    pad = lambda xs: xs + [None] * (k - len(xs))
    tops = _merge_top(rows[0], pad(rows[1]))
    rest = [x for r in rows[2:] for x in r]
    for base in range(0, len(rest), k):
        tops = _merge_top(tops, _sort_desc(pad(rest[base:base + k])))
    return tops


def _peer_route_body(q_ref, k1_ref, k2_ref, d1_ref, s2_ref, e1_ref, e2_ref):
    n_keys = k1_ref.shape[1]
    dq = k1_ref.shape[2]
    for h in range(k1_ref.shape[0]):
        q1 = q_ref[:, (2 * h) * dq:(2 * h + 1) * dq]
        q2 = q_ref[:, (2 * h + 1) * dq:(2 * h + 2) * dq]
        s1 = lax.dot_general(k1_ref[h], q1, _NT, preferred_element_type=F32)
        s2 = lax.dot_general(k2_ref[h], q2, _NT, preferred_element_type=F32)
        v1 = _top_keys(s1)
        v2 = _top_keys(s2)
        top = _grid_top(v1, v2)
        z = None
        for c in top:
            e = jnp.exp(c - top[0])
            z = e if z is None else z + e
        bc = lambda x: jnp.broadcast_to(x[0:1, :], (n_keys, x.shape[1]))
        d1_ref[h] = bc(top[PEER_TOPK - 1]) - s1
        s2_ref[h] = s2
        e1_ref[h] = jnp.exp(s1 - bc(v1[0]))
        e2_ref[h] = jnp.exp(s2 - bc(v2[0])) / bc(z)


def peer_route(q, k1, k2, *, tt):
    s, dqt = q.shape
    h, n_keys, dq = k1.shape
    out = jax.ShapeDtypeStruct((h, n_keys, s), F32)
    spec = pl.BlockSpec((h, n_keys, tt), lambda i: (0, 0, i))
    kspec = pl.BlockSpec((h, n_keys, dq), lambda i: (0, 0, 0))
    return pl.pallas_call(
        _peer_route_body,
        grid=(s // tt,),
        in_specs=[pl.BlockSpec((tt, dqt), lambda i: (i, 0)), kspec, kspec],
        out_specs=[spec] * 4,
        out_shape=[out] * 4,
        compiler_params=_params("parallel"),
        name="peer_route",
    )(q, k1, k2)


def _peer_mix_body(n_ref, u_ref, vt_ref, d1_ref, s2_ref, e1_ref, e2_ref, x_ref, o_ref,
                   acc_ref, at0_ref, at1_ref, h0_ref, h1_ref, *, n_blocks):
    i = pl.program_id(0)
    j = pl.program_id(1)
    te, tt = h0_ref.shape
    n_keys = s2_ref.shape[1]
    d = acc_ref.shape[0]
    n_chunks = te // n_keys
    half = tt // 2

    @pl.when((i == 0) & (j == 0))
    def _():
        at0_ref[...] = jnp.zeros(at0_ref.shape, F32)
        at1_ref[...] = jnp.zeros(at1_ref.shape, F32)
        h0_ref[...] = jnp.zeros(h0_ref.shape, h0_ref.dtype)
        h1_ref[...] = jnp.zeros(h1_ref.shape, h1_ref.dtype)

    @pl.when(j == 0)
    def _():
        acc_ref[...] = jnp.zeros(acc_ref.shape, F32)

    def stages(at_w, at_r, h_w, h_r, row0):
        weights_valid = (j >= 1) & (j <= n_blocks)
        kq = d // 4

        def act_piece(t0, r0, k0):
            def run():
                part = lax.dot_general(u_ref[r0:r0 + MXU_DIM, k0:k0 + kq], n_ref[t0:t0 + half, k0:k0 + kq],
                                       _NT, preferred_element_type=F32)
                if k0 == 0:
                    at_w[r0:r0 + MXU_DIM, t0:t0 + half] = part
                else:
                    at_w[r0:r0 + MXU_DIM, t0:t0 + half] += part
            return run

        def val_piece(o0, t0):
            def run():
                acc_ref[o0:o0 + MXU_DIM, t0:t0 + half] += jnp.dot(
                    vt_ref[o0:o0 + MXU_DIM, :], h_r[:, t0:t0 + half], preferred_element_type=F32)
            return run

        def weight_slab(c, k0):
            def run():
                cols = slice(c, c + LANES)
                keys = slice(k0, k0 + SLAB_KEYS)
                w = [jnp.zeros((SLAB_KEYS, LANES), F32) for _ in range(n_chunks)]
                for h in range(s2_ref.shape[0]):
                    s2 = s2_ref[h, keys, cols]
                    e2 = e2_ref[h, keys, cols]
                    for r in range(n_chunks):
                        d1 = d1_ref[h, row0 + r:row0 + r + 1, cols]
                        e1v = e1_ref[h, row0 + r:row0 + r + 1, cols]
                        w[r] = w[r] + jnp.where(s2 >= d1, e2 * e1v, 0.0)
                for r in range(n_chunks):
                    out = slice(r * n_keys + k0, r * n_keys + k0 + SLAB_KEYS)
                    hv = jax.nn.gelu(at_r[out, cols]) * w[r]
                    h_w[out, cols] = jnp.where(weights_valid, hv, 0.0).astype(h_w.dtype)
            return run

        acts = [act_piece(t0, r0, k0) for t0 in range(0, tt, half) for r0 in range(0, te, MXU_DIM)
                for k0 in range(0, d, kq)]
        vals = [val_piece(o0, t0) for o0 in range(0, d, MXU_DIM) for t0 in range(0, tt, half)]
        slabs = [weight_slab(c, k0) for c in range(0, tt, LANES) for k0 in range(0, n_keys, SLAB_KEYS)]
        mxu = [p for pair in zip(acts, vals) for p in pair]
        per = len(mxu) // len(slabs)
        for idx, slab in enumerate(slabs):
            for piece in mxu[idx * per:(idx + 1) * per]:
                piece()
            slab()

    @pl.when(j % 2 == 0)
    def _():
        stages(at0_ref, at1_ref, h1_ref, h0_ref, n_chunks)

    @pl.when(j % 2 == 1)
    def _():
        stages(at1_ref, at0_ref, h0_ref, h1_ref, 0)

    @pl.when(j == pl.num_programs(1) - 1)
    def _():
        o_ref[...] = x_ref[...] + acc_ref[...].T


def peer_mix(n, u_tab, vt_tab, route, x, *, tt, te):
    s, d = n.shape
    e = u_tab.shape[0]
    nb = e // te
    assert vt_tab.shape == (nb, d, te)
    d1, s2, e1, e2 = route
    h, n_keys, _ = s2.shape
    assert 2 * (te // n_keys) == SUBLANES
    clamp = lambda b: jnp.clip(b, 0, nb - 1)
    rspec = pl.BlockSpec((h, n_keys, tt), lambda i, j: (0, 0, i))
    gspec = pl.BlockSpec((h, SUBLANES, tt), lambda i, j: (0, clamp(j - 1) // 2, i))
    return pl.pallas_call(
        functools.partial(_peer_mix_body, n_blocks=nb),
        grid=(s // tt, nb + 2),
        in_specs=[
            pl.BlockSpec((tt, d), lambda i, j: (i, 0)),
            pl.BlockSpec((te, d), lambda i, j: (clamp(j), 0)),
            pl.BlockSpec((None, d, te), lambda i, j: (clamp(j - 2), 0, 0)),
            gspec, rspec, gspec, rspec,
            pl.BlockSpec((tt, d), lambda i, j: (i, 0)),
        ],
        out_specs=pl.BlockSpec((tt, d), lambda i, j: (i, 0)),
        out_shape=jax.ShapeDtypeStruct((s, d), F32),
        scratch_shapes=[
            pltpu.VMEM((d, tt), F32),
            pltpu.VMEM((te, tt), F32), pltpu.VMEM((te, tt), F32),
            pltpu.VMEM((te, tt), BF16), pltpu.VMEM((te, tt), BF16),
        ],
        compiler_params=_params("arbitrary", "arbitrary"),
        name="peer_mix",
    )(n, u_tab, vt_tab, *route, x)


def kernel(x, mem, norm_mix_g, w_in, ssm_a_re, ssm_a_im, ssm_log_step, ssm_b_re, ssm_b_im, ssm_c_re, ssm_c_im, ssm_d, ssm_w_glu, ssm_b_glu, ssm_out_g, att_lq1, att_lk1, att_lq2, att_lk2, att_subln_g, w_out, norm_mem_g, mem_norm_g, mem_w_q, mem_w_k, mem_w_v, mem_w_o, norm_ffn_g, peer_w_q, peer_k1, peer_k2, peer_u, peer_v, final_g):
    bsz, s, d = x.shape
    assert bsz == 1
    depth = w_in.shape[0]
    d_ssm = ssm_d.shape[1]
    d_att = (w_in.shape[2] - d_ssm) // 3
    n_att_heads = d_att // ATT_V_DIM
    tm = min(512, s)
    tp = min(1024, s)
    tn = 2048

    xs = x.reshape(s, d).astype(F32)
    mem2 = mem.reshape(mem.shape[1], d).astype(F32)
    for l in range(depth):
        proj = norm_matmul(xs, norm_mix_g[l], w_in[l].astype(BF16), tm=tp, tn=tn)
        y_ssm = s5_group(proj[:, :d_ssm], ssm_a_re[l], ssm_a_im[l], ssm_log_step[l], ssm_b_re[l], ssm_b_im[l],
                         ssm_c_re[l], ssm_c_im[l], ssm_d[l], ssm_w_glu[l], ssm_b_glu[l], ssm_out_g[l])
        lam_init = 0.8 - 0.6 * math.exp(-0.3 * l)
        lam = (jnp.exp(jnp.sum(att_lq1[l].astype(F32) * att_lk1[l].astype(F32)))
               - jnp.exp(jnp.sum(att_lq2[l].astype(F32) * att_lk2[l].astype(F32))) + lam_init)
        y_att = diff_attention(proj, lam, att_subln_g[l], n_heads=n_att_heads, q_col=d_ssm, k_col=d_ssm + d_att,
                               v_col=d_ssm + 2 * d_att, lam_init=lam_init, tq=tp)
        w_o = w_out[l].astype(BF16)
        xs = matmul_residual([y_ssm, y_att], [w_o[:d_ssm], w_o[d_ssm:]], xs, tm=tm, tn=tn)

        q = norm_matmul(xs, norm_mem_g[l], mem_w_q[l].astype(BF16), tm=tp, tn=tn)
        w_kv = jnp.concatenate([mem_w_k[l], mem_w_v[l]], axis=1).astype(BF16)
        kv = norm_matmul(mem2, mem_norm_g[l], w_kv, tm=mem2.shape[0], tn=tn)
        o = mem_attention(q, kv, n_heads=N_MEM_HEADS, tm=tp)
        xs = matmul_residual([o], [mem_w_o[l].astype(BF16)], xs, tm=tm, tn=tn)

        pq, n = norm_matmul(xs, norm_ffn_g[l], peer_w_q[l].astype(BF16), tm=tp, tn=tn, return_norm=True)
        route = peer_route(pq, peer_k1[l].astype(BF16), peer_k2[l].astype(BF16), tt=min(256, s))
        te = 512
        vt = peer_v[l].astype(BF16).reshape(-1, te, d).transpose(0, 2, 1)
        xs = peer_mix(n, peer_u[l].astype(BF16), vt, route, xs, tt=tm, te=te)
    return final_norm(xs, final_g, tm=tm).reshape(bsz, s, d)
```

```python
import functools
import math

import jax
import jax.numpy as jnp
from jax import lax
from jax.experimental import pallas as pl
from jax.experimental.pallas import tpu as pltpu

F32 = jnp.float32
BF16 = jnp.bfloat16

EPS = 1e-6
LANES = 128
SUBLANES = 8
MXU_DIM = 256
VMEM_LIMIT_BYTES = 56 * 1024 * 1024

CHUNK = 64
SSM_GROUP = 16
SSM_STATE = 64
SSM_BLOCK = 32
ATT_HEAD_DIM = 64
ATT_V_DIM = 2 * ATT_HEAD_DIM
N_MEM_HEADS = 4
PEER_HEADS = 8
PEER_N_KEYS = 128
PEER_D_HALF = 128
PEER_TOPK = 16
SLAB_KEYS = 32

_NT = (((1,), (1,)), ((), ()))


def _params(*sem):
    return pltpu.CompilerParams(dimension_semantics=sem, vmem_limit_bytes=VMEM_LIMIT_BYTES)


def _rms(x, g):
    return x * lax.rsqrt(jnp.mean(x * x, axis=-1, keepdims=True) + EPS) * g


def _norm_matmul_body(x_ref, g_ref, w_ref, o_ref, xn_ref):
    @pl.when(pl.program_id(1) == 0)
    def _():
        xn_ref[...] = _rms(x_ref[...], g_ref[...]).astype(xn_ref.dtype)

    o_ref[...] = jnp.dot(xn_ref[...], w_ref[...], preferred_element_type=F32).astype(o_ref.dtype)


def norm_matmul(x, g, w, *, tm, tn, return_norm=False):
    m, k = x.shape
    n = w.shape[1]
    out_specs = [pl.BlockSpec((tm, tn), lambda i, j: (i, j))]
    out_shape = [jax.ShapeDtypeStruct((m, n), BF16)]
    scratch = [pltpu.VMEM((tm, k), BF16)]
    if return_norm:
        out_specs.append(pl.BlockSpec((tm, k), lambda i, j: (i, 0)))
        out_shape.append(jax.ShapeDtypeStruct((m, k), BF16))
        scratch = []
    res = pl.pallas_call(
        _norm_matmul_body,
        grid=(m // tm, n // tn),
        in_specs=[
            pl.BlockSpec((tm, k), lambda i, j: (i, 0)),
            pl.BlockSpec((1, k), lambda i, j: (0, 0)),
            pl.BlockSpec((k, tn), lambda i, j: (0, j)),
        ],
        out_specs=out_specs,
        out_shape=out_shape,
        scratch_shapes=scratch,
        compiler_params=_params("parallel", "arbitrary"),
        name="norm_matmul",
    )(x, g.reshape(1, k).astype(F32), w)
    return res if return_norm else res[0]


def _matmul_residual_body(*refs, n_terms):
    x_ref, o_ref = refs[2 * n_terms], refs[2 * n_terms + 1]
    acc = x_ref[...]
    for y_ref, w_ref in zip(refs[:n_terms], refs[n_terms:2 * n_terms]):
        acc = acc + jnp.dot(y_ref[...], w_ref[...], preferred_element_type=F32)
    o_ref[...] = acc


def matmul_residual(ys, ws, x, *, tm, tn):
    m, n = x.shape
    in_specs = [pl.BlockSpec((tm, y.shape[1]), lambda i, j: (i, 0)) for y in ys]
    in_specs += [pl.BlockSpec((w.shape[0], tn), lambda i, j: (0, j)) for w in ws]
    in_specs += [pl.BlockSpec((tm, tn), lambda i, j: (i, j))]
    return pl.pallas_call(
        functools.partial(_matmul_residual_body, n_terms=len(ys)),
        grid=(m // tm, n // tn),
        in_specs=in_specs,
        out_specs=pl.BlockSpec((tm, tn), lambda i, j: (i, j)),
        out_shape=jax.ShapeDtypeStruct((m, n), F32),
        compiler_params=_params("parallel", "parallel"),
        name="matmul_residual",
    )(*ys, *ws, x)


def _final_norm_body(x_ref, g_ref, o_ref):
    o_ref[...] = _rms(x_ref[...], g_ref[...])


def final_norm(x, g, *, tm):
    m, k = x.shape
    return pl.pallas_call(
        _final_norm_body,
        grid=(m // tm,),
        in_specs=[pl.BlockSpec((tm, k), lambda i: (i, 0)), pl.BlockSpec((1, k), lambda i: (0, 0))],
        out_specs=pl.BlockSpec((tm, k), lambda i: (i, 0)),
        out_shape=jax.ShapeDtypeStruct((m, k), F32),
        compiler_params=_params("parallel"),
        name="final_norm",
    )(x, g.reshape(1, k).astype(F32))


def _s5_impulse_body(x_ref, b_ref, o_ref):
    o_ref[0] = jnp.dot(x_ref[0], b_ref[0], preferred_element_type=F32, precision=lax.Precision.HIGHEST)


def s5_impulse(x_ri, b_ri):
    g, lc, p2 = x_ri.shape
    c = b_ri.shape[2]
    return pl.pallas_call(
        _s5_impulse_body,
        grid=(g,),
        in_specs=[pl.BlockSpec((1, lc, p2), lambda i: (i, 0, 0)), pl.BlockSpec((1, p2, c), lambda i: (i, 0, 0))],
        out_specs=pl.BlockSpec((1, lc, c), lambda i: (i, 0, 0)),
        out_shape=jax.ShapeDtypeStruct((g, lc, c), F32),
        compiler_params=_params("parallel"),
        name="s5_impulse",
    )(x_ri, b_ri)


def _s5_block_input_body(u_ref, bx_ref, z_ref):
    z_ref[0] = jnp.dot(u_ref[0], bx_ref[0], preferred_element_type=F32)


def s5_block_input(ug, bx):
    g, nb, lc = ug.shape
    p2 = bx.shape[2]
    return pl.pallas_call(
        _s5_block_input_body,
        grid=(g,),
        in_specs=[pl.BlockSpec((1, nb, lc), lambda i: (i, 0, 0)), pl.BlockSpec((1, lc, p2), lambda i: (i, 0, 0))],
        out_specs=pl.BlockSpec((1, nb, p2), lambda i: (i, 0, 0)),
        out_shape=jax.ShapeDtypeStruct((g, nb, p2), F32),
        compiler_params=_params("parallel"),
        name="s5_block_input",
    )(ug, bx)


def _s5_scan_body(z_ref, a1_ref, a2_ref, h_ref, state_ref, *, half):
    @pl.when(pl.program_id(0) == 0)
    def _():
        state_ref[...] = jnp.zeros(state_ref.shape, F32)

    a1 = a1_ref[...]
    a2 = a2_ref[...]

    def step(k, state):
        h_ref[k] = state
        return state * a1 + pltpu.roll(state, half, 1) * a2 + z_ref[k]

    state_ref[...] = lax.fori_loop(0, z_ref.shape[0], step, state_ref[...])


def s5_scan(zt, a1, a2, *, tb):
    nb, g, p2 = zt.shape
    return pl.pallas_call(
        functools.partial(_s5_scan_body, half=p2 // 2),
        grid=(nb // tb,),
        in_specs=[
            pl.BlockSpec((tb, g, p2), lambda i: (i, 0, 0)),
            pl.BlockSpec((g, p2), lambda i: (0, 0)),
            pl.BlockSpec((g, p2), lambda i: (0, 0)),
        ],
        out_specs=pl.BlockSpec((tb, g, p2), lambda i: (i, 0, 0)),
        out_shape=jax.ShapeDtypeStruct((nb, g, p2), F32),
        scratch_shapes=[pltpu.VMEM((g, p2), F32)],
        compiler_params=_params("arbitrary"),
        name="s5_scan",
    )(zt, a1, a2)


def _s5_output_body(u_ref, mt_ref, h_ref, cx_ref, d_ref, y_ref):
    u = u_ref[0]
    y = jnp.dot(u, mt_ref[0], preferred_element_type=F32)
    y = y + jnp.dot(h_ref[0], cx_ref[0], preferred_element_type=F32)
    y = y + d_ref[0] * u.astype(F32)
    y_ref[0] = jax.nn.gelu(y).astype(y_ref.dtype)


def s5_output(ug, mt, hg, cx, dg):
    g, nb, lc = ug.shape
    p2 = hg.shape[2]
    return pl.pallas_call(
        _s5_output_body,
        grid=(g,),
        in_specs=[
            pl.BlockSpec((1, nb, lc), lambda i: (i, 0, 0)),
            pl.BlockSpec((1, lc, lc), lambda i: (i, 0, 0)),
            pl.BlockSpec((1, nb, p2), lambda i: (i, 0, 0)),
            pl.BlockSpec((1, p2, lc), lambda i: (i, 0, 0)),
            pl.BlockSpec((1, 1, lc), lambda i: (i, 0, 0)),
        ],
        out_specs=pl.BlockSpec((1, nb, lc), lambda i: (i, 0, 0)),
        out_shape=jax.ShapeDtypeStruct((g, nb, lc), BF16),
        compiler_params=_params("parallel"),
        name="s5_output",
    )(ug, mt, hg, cx, dg)


def _glu_norm_body(y_ref, w_ref, b_ref, g_ref, o_ref):
    y = y_ref[...]
    z = jnp.dot(y, w_ref[...], preferred_element_type=F32) + b_ref[...]
    o = y.astype(F32) * jax.nn.sigmoid(z)
    o_ref[...] = _rms(o, g_ref[...]).astype(o_ref.dtype)


def glu_norm(y, w, b, g, *, tm):
    m, k = y.shape
    return pl.pallas_call(
        _glu_norm_body,
        grid=(m // tm,),
        in_specs=[
            pl.BlockSpec((tm, k), lambda i: (i, 0)),
            pl.BlockSpec((k, k), lambda i: (0, 0)),
            pl.BlockSpec((1, k), lambda i: (0, 0)),
            pl.BlockSpec((1, k), lambda i: (0, 0)),
        ],
        out_specs=pl.BlockSpec((tm, k), lambda i: (i, 0)),
        out_shape=jax.ShapeDtypeStruct((m, k), BF16),
        compiler_params=_params("parallel"),
        name="glu_norm",
    )(y, w, b.reshape(1, k).astype(F32), g.reshape(1, k).astype(F32))


def s5_group(u, a_re, a_im, log_step, b_re, b_im, c_re, c_im, d_skip, w_glu, b_glu, out_g):
    s, d_ssm = u.shape
    g, p = a_re.shape
    c = SSM_GROUP
    blk = SSM_BLOCK
    nb = s // blk
    lc = blk * c

    a = lax.complex(a_re.astype(F32), a_im.astype(F32))
    step = jnp.exp(log_step.astype(F32))[:, None]
    log_abar = a * step
    a_bar = jnp.exp(log_abar)
    b_bar = ((a_bar - 1.0) / a)[..., None] * lax.complex(b_re.astype(F32), b_im.astype(F32))
    cm = lax.complex(c_re.astype(F32), c_im.astype(F32))
    pos = jnp.arange(blk, dtype=F32)

    def apow(e):
        return jnp.exp(log_abar[:, None, :] * e[None, :, None])

    xm = (apow(pos)[:, :, None, :] * cm[:, None, :, :]).reshape(g, lc, p)
    x_ri = jnp.concatenate([jnp.real(xm), -jnp.imag(xm)], axis=2)
    b_ri = jnp.concatenate([jnp.real(b_bar), jnp.imag(b_bar)], axis=1)
    k4 = s5_impulse(x_ri, b_ri).reshape(g, blk, c, c)
    lag = jnp.arange(blk)[None, :] - jnp.arange(blk)[:, None]
    mt = jnp.where((lag >= 0)[None, :, :, None, None], k4[:, jnp.maximum(lag, 0)], 0.0)
    mt = mt.transpose(0, 1, 4, 2, 3).reshape(g, lc, lc).astype(BF16)

    bxm = (apow(blk - 1.0 - pos)[:, :, None, :] * jnp.swapaxes(b_bar, 1, 2)[:, None, :, :]).reshape(g, lc, p)
    bx = jnp.concatenate([jnp.real(bxm), jnp.imag(bxm)], axis=2).astype(BF16)
    cxm = jnp.swapaxes((apow(pos + 1.0)[:, :, None, :] * cm[:, None, :, :]).reshape(g, lc, p), 1, 2)
    cx = jnp.concatenate([jnp.real(cxm), -jnp.imag(cxm)], axis=1).astype(BF16)
    a_blk = jnp.exp(log_abar * blk)
    a1 = jnp.concatenate([jnp.real(a_blk), jnp.real(a_blk)], axis=1)
    a2 = jnp.concatenate([-jnp.imag(a_blk), jnp.imag(a_blk)], axis=1)
    dg = jnp.tile(d_skip.astype(F32).reshape(g, 1, c), (1, blk, 1)).reshape(g, 1, lc)

    ug = u.reshape(nb, blk, g, c).transpose(2, 0, 1, 3).reshape(g, nb, lc)
    z = s5_block_input(ug, bx)
    h = s5_scan(z.transpose(1, 0, 2), a1, a2, tb=min(64, nb))
    hg = h.transpose(1, 0, 2).astype(BF16)
    yg = s5_output(ug, mt, hg, cx, dg)
    y = yg.reshape(g, nb, blk, c).transpose(1, 2, 0, 3).reshape(s, d_ssm)
    return glu_norm(y, w_glu.astype(BF16), b_glu, out_g, tm=min(512, s))


def _diff_attn_body(lam_ref, q_ref, k_ref, v_ref, g_ref, o_ref, m_scr, l_scr, acc_scr, *, tq, out_scale):
    i = pl.program_id(1)
    q = (q_ref[...].astype(F32) * (ATT_HEAD_DIM ** -0.5 * math.log2(math.e))).astype(q_ref.dtype)
    lane = lax.broadcasted_iota(jnp.int32, q.shape, 1)
    zero = jnp.zeros_like(q)
    qq = jnp.concatenate([jnp.where(lane < ATT_HEAD_DIM, q, zero), jnp.where(lane >= ATT_HEAD_DIM, q, zero)], axis=0)

    m_scr[...] = jnp.full(m_scr.shape, -jnp.inf, F32)
    l_scr[...] = jnp.zeros(l_scr.shape, F32)
    acc_scr[...] = jnp.zeros(acc_scr.shape, F32)
    reps = tq // LANES

    def step(start, masked):
        kb = k_ref[pl.ds(start, tq), :]
        vb = v_ref[pl.ds(start, tq), :]
        s = lax.dot_general(qq, kb, _NT, preferred_element_type=F32)
        if masked:
            qc = (lax.broadcasted_iota(jnp.int32, s.shape, 0) % tq) // CHUNK
            kc = lax.broadcasted_iota(jnp.int32, s.shape, 1) // CHUNK
            s = jnp.where(kc <= qc, s, -jnp.inf)
        m_prev = m_scr[...]
        m_next = jnp.maximum(m_prev, jnp.max(s, axis=1, keepdims=True))
        alpha = jnp.exp2(m_prev - m_next)
        p = jnp.exp2(s - jnp.concatenate([m_next] * reps, axis=1))
        l_scr[...] = alpha * l_scr[...] + jnp.sum(p, axis=1, keepdims=True)
        acc_scr[...] = alpha * acc_scr[...] + jnp.dot(p.astype(vb.dtype), vb, preferred_element_type=F32)
        m_scr[...] = m_next

    def full_pair(j, carry):
        step(pl.multiple_of(2 * j * tq, tq), False)
        step(pl.multiple_of((2 * j + 1) * tq, tq), False)
        return carry

    lax.fori_loop(0, lax.shift_right_logical(i, 1), full_pair, 0)
    diag = pl.multiple_of(i * tq, tq)

    @pl.when((i & 1) == 1)
    def _():
        step(pl.multiple_of((i - 1) * tq, tq), False)
        step(diag, True)

    @pl.when((i & 1) == 0)
    def _():
        step(diag, True)

    o = acc_scr[...] / l_scr[...]
    d = o[:tq] - lam_ref[0] * o[tq:]
    o_ref[...] = (_rms(d, g_ref[...]) * out_scale).astype(o_ref.dtype)


def diff_attention(proj, lam, subln_g, *, n_heads, q_col, k_col, v_col, lam_init, tq):
    s = proj.shape[0]
    dv = ATT_V_DIM
    return pl.pallas_call(
        functools.partial(_diff_attn_body, tq=tq, out_scale=1.0 - lam_init),
        grid=(n_heads, s // tq),
        in_specs=[
            pl.BlockSpec(memory_space=pltpu.SMEM),
            pl.BlockSpec((tq, dv), lambda h, i: (i, q_col // dv + h)),
            pl.BlockSpec((s, dv), lambda h, i: (0, k_col // dv + h)),
            pl.BlockSpec((s, dv), lambda h, i: (0, v_col // dv + h)),
            pl.BlockSpec((1, dv), lambda h, i: (0, 0)),
        ],
        out_specs=pl.BlockSpec((tq, dv), lambda h, i: (i, h)),
        out_shape=jax.ShapeDtypeStruct((s, n_heads * dv), BF16),
        scratch_shapes=[
            pltpu.VMEM((2 * tq, LANES), F32),
            pltpu.VMEM((2 * tq, LANES), F32),
            pltpu.VMEM((2 * tq, dv), F32),
        ],
        compiler_params=_params("parallel", "arbitrary"),
        name="diff_attention",
    )(lam.reshape(1).astype(F32), proj, proj, proj, subln_g.reshape(1, dv).astype(F32))


def _mem_attn_body(q_ref, k_ref, v_ref, o_ref, *, scale):
    s = lax.dot_general(q_ref[...], k_ref[...], _NT, preferred_element_type=F32) * scale
    p = jnp.exp(s - jnp.max(s, axis=1, keepdims=True))
    p = p / jnp.sum(p, axis=1, keepdims=True)
    o_ref[...] = jnp.dot(p.astype(v_ref.dtype), v_ref[...], preferred_element_type=F32).astype(o_ref.dtype)


def mem_attention(q, kv, *, n_heads, tm):
    s, d = q.shape
    m = kv.shape[0]
    dh = d // n_heads
    return pl.pallas_call(
        functools.partial(_mem_attn_body, scale=dh ** -0.5),
        grid=(s // tm, n_heads),
        in_specs=[
            pl.BlockSpec((tm, dh), lambda i, h: (i, h)),
            pl.BlockSpec((m, dh), lambda i, h: (0, h)),
            pl.BlockSpec((m, dh), lambda i, h: (0, n_heads + h)),
        ],
        out_specs=pl.BlockSpec((tm, dh), lambda i, h: (i, h)),
        out_shape=jax.ShapeDtypeStruct((s, d), BF16),
        compiler_params=_params("parallel", "parallel"),
        name="mem_attention",
    )(q, kv, kv)


def _cmpx(xs, i, j):
    a, b = xs[i], xs[j]
    if b is None:
        return
    if a is None:
        xs[i], xs[j] = b, None
        return
    xs[i], xs[j] = jnp.maximum(a, b), jnp.minimum(a, b)


def _bitonic_merge_desc(xs):
    n = len(xs)
    j = n // 2
    while j >= 1:
        for i in range(n):
            if (i ^ j) > i:
                _cmpx(xs, i, i ^ j)
        j //= 2
    return xs


def _sort_desc(xs):
    n = len(xs)
    k = 2
    while k <= n:
        j = k // 2
        while j >= 1:
            for i in range(n):
                l = i ^ j
                if l > i:
                    if (i & k) == 0:
                        _cmpx(xs, i, l)
                    else:
                        _cmpx(xs, l, i)
            j //= 2
        k *= 2
    return xs


def _max_or_none(a, b):
    if a is None:
        return b
    if b is None:
        return a
    return jnp.maximum(a, b)


def _merge_top(a, b):
    n = len(a)
    return _bitonic_merge_desc([_max_or_none(a[i], b[n - 1 - i]) for i in range(n)])


def _top_keys(st):
    n_keys = st.shape[0]
    k = PEER_TOPK
    parts = [st[r * SUBLANES:(r + 1) * SUBLANES, :] for r in range(n_keys // SUBLANES)]
    tops = None
    for base in range(0, len(parts), k):
        grp = _sort_desc(parts[base:base + k])
        tops = grp if tops is None else _merge_top(tops, grp)
    shift = SUBLANES // 2
    while shift >= 1:
        tops = _merge_top(tops, [pltpu.roll(x, shift, 0) for x in tops])
        shift //= 2
    return tops


def _grid_top(v1, v2):
    k = PEER_TOPK
    rows = [[v1[i] + v2[j] for j in range(k // (i + 1))] for i in range(k)]
    pad = lambda xs: xs + [None] * (k - len(xs))
    tops = _merge_top(rows[0], pad(rows[1]))
    rest = [x for r in rows[2:] for x in r]
    for base in range(0, len(rest), k):
        tops = _merge_top(tops, _sort_desc(pad(rest[base:base + k])))
    return tops


def _peer_route_body(q_ref, k1_ref, k2_ref, d1_ref, s2_ref, e1_ref, e2_ref):
    n_keys = k1_ref.shape[1]
    dq = k1_ref.shape[2]
    for h in range(k1_ref.shape[0]):
        q1 = q_ref[:, (2 * h) * dq:(2 * h + 1) * dq]
        q2 = q_ref[:, (2 * h + 1) * dq:(2 * h + 2) * dq]
        s1 = lax.dot_general(k1_ref[h], q1, _NT, preferred_element_type=F32)
        s2 = lax.dot_general(k2_ref[h], q2, _NT, preferred_element_type=F32)
        v1 = _top_keys(s1)
        v2 = _top_keys(s2)
        top = _grid_top(v1, v2)
        z = None
        for c in top:
            e = jnp.exp(c - top[0])
            z = e if z is None else z + e
        bc = lambda x: jnp.broadcast_to(x[0:1, :], (n_keys, x.shape[1]))
        d1_ref[h] = bc(top[PEER_TOPK - 1]) - s1
        s2_ref[h] = s2
        e1_ref[h] = jnp.exp(s1 - bc(v1[0]))
        e2_ref[h] = jnp.exp(s2 - bc(v2[0])) / bc(z)


def peer_route(q, k1, k2, *, tt):
    s, dqt = q.shape
    h, n_keys, dq = k1.shape
    out = jax.ShapeDtypeStruct((h, n_keys, s), F32)
    spec = pl.BlockSpec((h, n_keys, tt), lambda i: (0, 0, i))
    kspec = pl.BlockSpec((h, n_keys, dq), lambda i: (0, 0, 0))
    return pl.pallas_call(
        _peer_route_body,
        grid=(s // tt,),
        in_specs=[pl.BlockSpec((tt, dqt), lambda i: (i, 0)), kspec, kspec],
        out_specs=[spec] * 4,
        out_shape=[out] * 4,
        compiler_params=_params("parallel"),
        name="peer_route",
    )(q, k1, k2)


def _peer_mix_body(n_ref, u_ref, vt_ref, d1_ref, s2_ref, e1_ref, e2_ref, x_ref, o_ref,
                   acc_ref, at0_ref, at1_ref, h0_ref, h1_ref, *, n_blocks):
    i = pl.program_id(0)
    j = pl.program_id(1)
    te, tt = h0_ref.shape
    n_keys = s2_ref.shape[1]
    d = acc_ref.shape[0]
    n_chunks = te // n_keys
    half = tt // 2

    @pl.when((i == 0) & (j == 0))
    def _():
        at0_ref[...] = jnp.zeros(at0_ref.shape, F32)
        at1_ref[...] = jnp.zeros(at1_ref.shape, F32)
        h0_ref[...] = jnp.zeros(h0_ref.shape, h0_ref.dtype)
        h1_ref[...] = jnp.zeros(h1_ref.shape, h1_ref.dtype)

    @pl.when(j == 0)
    def _():
        acc_ref[...] = jnp.zeros(acc_ref.shape, F32)

    def stages(at_w, at_r, h_w, h_r, row0):
        weights_valid = (j >= 1) & (j <= n_blocks)
        kq = d // 4

        def act_piece(t0, r0, k0):
            def run():
                part = lax.dot_general(u_ref[r0:r0 + MXU_DIM, k0:k0 + kq], n_ref[t0:t0 + half, k0:k0 + kq],
                                       _NT, preferred_element_type=F32)
                if k0 == 0:
                    at_w[r0:r0 + MXU_DIM, t0:t0 + half] = part
                else:
                    at_w[r0:r0 + MXU_DIM, t0:t0 + half] += part
            return run

        def val_piece(o0, t0):
            def run():
                acc_ref[o0:o0 + MXU_DIM, t0:t0 + half] += jnp.dot(
                    vt_ref[o0:o0 + MXU_DIM, :], h_r[:, t0:t0 + half], preferred_element_type=F32)
            return run

        def weight_slab(c, k0):
            def run():
                cols = slice(c, c + LANES)
                keys = slice(k0, k0 + SLAB_KEYS)
                w = [jnp.zeros((SLAB_KEYS, LANES), F32) for _ in range(n_chunks)]
                for h in range(s2_ref.shape[0]):
                    s2 = s2_ref[h, keys, cols]
                    e2 = e2_ref[h, keys, cols]
                    for r in range(n_chunks):
                        d1 = d1_ref[h, row0 + r:row0 + r + 1, cols]
                        e1v = e1_ref[h, row0 + r:row0 + r + 1, cols]
                        w[r] = w[r] + jnp.where(s2 >= d1, e2 * e1v, 0.0)
                for r in range(n_chunks):
                    out = slice(r * n_keys + k0, r * n_keys + k0 + SLAB_KEYS)
                    hv = jax.nn.gelu(at_r[out, cols]) * w[r]
                    h_w[out, cols] = jnp.where(weights_valid, hv, 0.0).astype(h_w.dtype)
            return run

        acts = [act_piece(t0, r0, k0) for t0 in range(0, tt, half) for r0 in range(0, te, MXU_DIM)
                for k0 in range(0, d, kq)]
        vals = [val_piece(o0, t0) for o0 in range(0, d, MXU_DIM) for t0 in range(0, tt, half)]
        slabs = [weight_slab(c, k0) for c in range(0, tt, LANES) for k0 in range(0, n_keys, SLAB_KEYS)]
        mxu = [p for pair in zip(acts, vals) for p in pair]
        per = len(mxu) // len(slabs)
        for idx, slab in enumerate(slabs):
            for piece in mxu[idx * per:(idx + 1) * per]:
                piece()
            slab()

    @pl.when(j % 2 == 0)
    def _():
        stages(at0_ref, at1_ref, h1_ref, h0_ref, n_chunks)

    @pl.when(j % 2 == 1)
    def _():
        stages(at1_ref, at0_ref, h0_ref, h1_ref, 0)

    @pl.when(j == pl.num_programs(1) - 1)
    def _():
        o_ref[...] = x_ref[...] + acc_ref[...].T


def peer_mix(n, u_tab, vt_tab, route, x, *, tt, te):
    s, d = n.shape
    e = u_tab.shape[0]
    nb = e // te
    assert vt_tab.shape == (nb, d, te)
    d1, s2, e1, e2 = route
    h, n_keys, _ = s2.shape
    assert 2 * (te // n_keys) == SUBLANES
    clamp = lambda b: jnp.clip(b, 0, nb - 1)
    rspec = pl.BlockSpec((h, n_keys, tt), lambda i, j: (0, 0, i))
    gspec = pl.BlockSpec((h, SUBLANES, tt), lambda i, j: (0, clamp(j - 1) // 2, i))
    return pl.pallas_call(
        functools.partial(_peer_mix_body, n_blocks=nb),
        grid=(s // tt, nb + 2),
        in_specs=[
            pl.BlockSpec((tt, d), lambda i, j: (i, 0)),
            pl.BlockSpec((te, d), lambda i, j: (clamp(j), 0)),
            pl.BlockSpec((None, d, te), lambda i, j: (clamp(j - 2), 0, 0)),
            gspec, rspec, gspec, rspec,
            pl.BlockSpec((tt, d), lambda i, j: (i, 0)),
        ],
        out_specs=pl.BlockSpec((tt, d), lambda i, j: (i, 0)),
        out_shape=jax.ShapeDtypeStruct((s, d), F32),
        scratch_shapes=[
            pltpu.VMEM((d, tt), F32),
            pltpu.VMEM((te, tt), F32), pltpu.VMEM((te, tt), F32),
            pltpu.VMEM((te, tt), BF16), pltpu.VMEM((te, tt), BF16),
        ],
        compiler_params=_params("arbitrary", "arbitrary"),
        name="peer_mix",
    )(n, u_tab, vt_tab, *route, x)


def kernel(x, mem, norm_mix_g, w_in, ssm_a_re, ssm_a_im, ssm_log_step, ssm_b_re, ssm_b_im, ssm_c_re, ssm_c_im, ssm_d, ssm_w_glu, ssm_b_glu, ssm_out_g, att_lq1, att_lk1, att_lq2, att_lk2, att_subln_g, w_out, norm_mem_g, mem_norm_g, mem_w_q, mem_w_k, mem_w_v, mem_w_o, norm_ffn_g, peer_w_q, peer_k1, peer_k2, peer_u, peer_v, final_g):
    bsz, s, d = x.shape
    assert bsz == 1
    depth = w_in.shape[0]
    d_ssm = ssm_d.shape[1]
    d_att = (w_in.shape[2] - d_ssm) // 3
    n_att_heads = d_att // ATT_V_DIM
    tm = min(512, s)
    tp = min(1024, s)
    tn = 2048

    xs = x.reshape(s, d).astype(F32)
    mem2 = mem.reshape(mem.shape[1], d).astype(F32)
    for l in range(depth):
        proj = norm_matmul(xs, norm_mix_g[l], w_in[l].astype(BF16), tm=tp, tn=tn)
        y_ssm = s5_group(proj[:, :d_ssm], ssm_a_re[l], ssm_a_im[l], ssm_log_step[l], ssm_b_re[l], ssm_b_im[l],
                         ssm_c_re[l], ssm_c_im[l], ssm_d[l], ssm_w_glu[l], ssm_b_glu[l], ssm_out_g[l])
        lam_init = 0.8 - 0.6 * math.exp(-0.3 * l)
        lam = (jnp.exp(jnp.sum(att_lq1[l].astype(F32) * att_lk1[l].astype(F32)))
               - jnp.exp(jnp.sum(att_lq2[l].astype(F32) * att_lk2[l].astype(F32))) + lam_init)
        y_att = diff_attention(proj, lam, att_subln_g[l], n_heads=n_att_heads, q_col=d_ssm, k_col=d_ssm + d_att,
                               v_col=d_ssm + 2 * d_att, lam_init=lam_init, tq=tp)
        w_o = w_out[l].astype(BF16)
        xs = matmul_residual([y_ssm, y_att], [w_o[:d_ssm], w_o[d_ssm:]], xs, tm=tm, tn=tn)

        q = norm_matmul(xs, norm_mem_g[l], mem_w_q[l].astype(BF16), tm=tp, tn=tn)
        w_kv = jnp.concatenate([mem_w_k[l], mem_w_v[l]], axis=1).astype(BF16)
        kv = norm_matmul(mem2, mem_norm_g[l], w_kv, tm=mem2.shape[0], tn=tn)
        o = mem_attention(q, kv, n_heads=N_MEM_HEADS, tm=tp)
        xs = matmul_residual([o], [mem_w_o[l].astype(BF16)], xs, tm=tm, tn=tn)

        pq, n = norm_matmul(xs, norm_ffn_g[l], peer_w_q[l].astype(BF16), tm=tp, tn=tn, return_norm=True)
        route = peer_route(pq, peer_k1[l].astype(BF16), peer_k2[l].astype(BF16), tt=min(256, s))
        te = 512
        vt = peer_v[l].astype(BF16).reshape(-1, te, d).transpose(0, 2, 1)
        xs = peer_mix(n, peer_u[l].astype(BF16), vt, route, xs, tt=tm, te=te)
    return final_norm(xs, final_g, tm=tm).reshape(bsz, s, d)
```
